```python
import math
import jax, jax.numpy as jnp
from jax import lax
import numpy as np

D_MODEL = 2048
BATCH = 4
SEQ = 2048
DEPTH = 1
DEC_BATCH = 128
DEC_SEQ = 1
PAST_LEN = 16384
PAGE_SIZE = 128

CONV_W = 4
D_LRU = D_MODEL
LRU_BLOCKS = 16
LRU_BLOCK = D_LRU // LRU_BLOCKS
LRU_C = 8.0
D_SSD = D_MODEL
SSD_HEAD_DIM = 64
SSD_HEADS = D_SSD // SSD_HEAD_DIM
SSD_GROUPS = 4
SSD_HPG = SSD_HEADS // SSD_GROUPS
SSD_STATE = 128
SSD_CHUNK = 128
D_XBC = D_SSD + 2 * SSD_GROUPS * SSD_STATE
D_MIX = D_LRU + D_SSD
D_IN_PROJ = 2 * D_LRU + D_SSD + D_XBC + SSD_HEADS
MEM_LEN = 256
MEM_HEADS = 4
MEM_HEAD_DIM = D_MODEL // MEM_HEADS
PEER_HEADS = 8
PEER_NKEYS = 128
PEER_EXPERTS = PEER_NKEYS * PEER_NKEYS
PEER_DKEY = 256
PEER_HALF = PEER_DKEY // 2
PEER_TOPK = 16
PEER_BLOCK = 256
DN_ALPHA = (2.0 * DEPTH) ** 0.25
DN_BETA = (8.0 * DEPTH) ** -0.25
LN_EPS = 1e-5
RMS_EPS = 1e-6

kernel_name = 'hymba_rglru_ssd_peer_memxattn_step'


def layer_norm(x, g, b):
    xf = x.astype(jnp.float32)
    mu = jnp.mean(xf, -1, keepdims=True)
    var = jnp.mean(jnp.square(xf - mu), -1, keepdims=True)
    return ((xf - mu) * lax.rsqrt(var + LN_EPS) * g + b).astype(x.dtype)


def rms_norm(x, g):
    xf = x.astype(jnp.float32)
    return (xf * lax.rsqrt(jnp.mean(xf * xf, -1, keepdims=True) + RMS_EPS) * g).astype(x.dtype)


def causal_conv(x, buf, w, b):
    T = x.shape[1]
    xp = jnp.concatenate([buf.astype(x.dtype), x], axis=1)
    y = b + w[0] * xp[:, 0:T]
    for k in range(1, CONV_W):
        y = y + w[k] * xp[:, k:k + T]
    return y, xp[:, T:]


def rg_lru(xc, h0, wa, ba, wx, bx, lam):
    Bsz, T, _ = xc.shape
    xb = xc.reshape(Bsz, T, LRU_BLOCKS, LRU_BLOCK)
    r = jax.nn.sigmoid(jnp.einsum('btki,kij->btkj', xb, wa).reshape(Bsz, T, D_LRU) + ba)
    i = jax.nn.sigmoid(jnp.einsum('btki,kij->btkj', xb, wx).reshape(Bsz, T, D_LRU) + bx)
    log_a = -LRU_C * r.astype(jnp.float32) * jax.nn.softplus(-lam.astype(jnp.float32))
    a = jnp.exp(log_a)
    mult = jnp.sqrt(jnp.maximum(-jnp.expm1(2.0 * log_a), 0.0))
    u = mult * (i * xc).astype(jnp.float32)
    u = u.at[:, 0].add(a[:, 0] * h0.astype(jnp.float32))

    def combine(c1, c2):
        a1, b1 = c1
        a2, b2 = c2
        return a1 * a2, a2 * b1 + b2

    _, h = lax.associative_scan(combine, (a, u), axis=1)
    return h.astype(xc.dtype), h[:, -1].astype(xc.dtype)


def segsum(x):
    T = x.shape[-1]
    xr = jnp.broadcast_to(x[..., :, None], x.shape + (T,))
    strict = jnp.tril(jnp.ones((T, T), bool), -1)
    cs = jnp.cumsum(jnp.where(strict, xr, 0.0), axis=-2)
    return jnp.where(jnp.tril(jnp.ones((T, T), bool)), cs, -jnp.inf)


def ssd_scan(x, dt, A, Bm, Cm, h0):
    Bsz, T = x.shape[:2]
    Q = min(SSD_CHUNK, T)
    nc = -(-T // Q)
    pad = nc * Q - T

    def padt(z):
        return jnp.pad(z, [(0, 0), (0, pad)] + [(0, 0)] * (z.ndim - 2))

    x, dt, Bm, Cm = padt(x), padt(dt), padt(Bm), padt(Cm)
    xg = x.reshape(Bsz, nc, Q, SSD_GROUPS, SSD_HPG, SSD_HEAD_DIM)
    dtg = dt.reshape(Bsz, nc, Q, SSD_GROUPS, SSD_HPG)
    Bc = Bm.reshape(Bsz, nc, Q, SSD_GROUPS, SSD_STATE)
    Cc = Cm.reshape(Bsz, nc, Q, SSD_GROUPS, SSD_STATE)
    Ag = jnp.moveaxis(dtg * A.reshape(SSD_GROUPS, SSD_HPG), 2, -1)
    A_cs = jnp.cumsum(Ag, axis=-1)
    L = jnp.exp(segsum(Ag))
    xdt = xg * dtg[..., None]
    CB = jnp.einsum('bclgn,bcsgn->bcgls', Cc, Bc)
    y_diag = jnp.einsum('bcgrls,bcsgrp->bclgrp', CB[:, :, :, None] * L, xdt)
    decay_states = jnp.exp(A_cs[..., -1:] - A_cs)
    states = jnp.einsum('bclgn,bcgrl,bclgrp->bcgrpn', Bc, decay_states, xdt)
    h0g = h0.reshape(Bsz, SSD_GROUPS, SSD_HPG, SSD_HEAD_DIM, SSD_STATE)
    states = jnp.concatenate([h0g[:, None], states], axis=1)
    chunk_A = jnp.pad(A_cs[..., -1], [(0, 0), (1, 0), (0, 0), (0, 0)])
    decay_chunk = jnp.exp(segsum(jnp.moveaxis(chunk_A, 1, -1)))
    new_states = jnp.einsum('bgrzc,bcgrpn->bzgrpn', decay_chunk, states)
    states, h_last = new_states[:, :-1], new_states[:, -1]
    y_off = jnp.einsum('bclgn,bcgrpn,bcgrl->bclgrp', Cc, states, jnp.exp(A_cs))
    y = (y_diag + y_off).reshape(Bsz, nc * Q, SSD_HEADS, SSD_HEAD_DIM)[:, :T]
    return y, h_last.reshape(Bsz, SSD_HEADS, SSD_HEAD_DIM, SSD_STATE)


def parallel_mixer(x, lru_buf, lru_h0, ssd_buf, ssd_h0, w):
    Bsz, T, _ = x.shape
    proj = x @ w['w_in']
    lru_x, lru_gate, z, xbc, dt = jnp.split(
        proj, [D_LRU, 2 * D_LRU, 2 * D_LRU + D_SSD, 2 * D_LRU + D_SSD + D_XBC], axis=-1)
    xc, lru_buf_new = causal_conv(lru_x, lru_buf, w['lru_conv_w'], w['lru_conv_b'])
    h, lru_h_new = rg_lru(xc, lru_h0, w['lru_gate_a_w'], w['lru_gate_a_b'],
                          w['lru_gate_x_w'], w['lru_gate_x_b'], w['lru_lambda'])
    y_lru = rms_norm(h * jax.nn.gelu(lru_gate, approximate=False), w['lru_norm_g'])
    xbc_c, ssd_buf_new = causal_conv(xbc, ssd_buf, w['ssd_conv_w'], w['ssd_conv_b'])
    xbc_c = jax.nn.silu(xbc_c)
    xs, Bm, Cm = jnp.split(xbc_c, [D_SSD, D_SSD + SSD_GROUPS * SSD_STATE], axis=-1)
    xs4 = xs.reshape(Bsz, T, SSD_HEADS, SSD_HEAD_DIM).astype(jnp.float32)
    dtp = jax.nn.softplus(dt.astype(jnp.float32) + w['ssd_dt_bias'])
    A = -jnp.exp(w['ssd_A_log'].astype(jnp.float32))
    y, ssd_h_new = ssd_scan(xs4, dtp, A,
                            Bm.reshape(Bsz, T, SSD_GROUPS, SSD_STATE).astype(jnp.float32),
                            Cm.reshape(Bsz, T, SSD_GROUPS, SSD_STATE).astype(jnp.float32),
                            ssd_h0.astype(jnp.float32))
    y = y + w['ssd_D'][:, None] * xs4
    y_ssd = rms_norm(y.reshape(Bsz, T, D_SSD).astype(x.dtype) * jax.nn.silu(z), w['ssd_norm_g'])
    out = jnp.concatenate([y_lru, y_ssd], axis=-1) @ w['w_out']
    return out, (lru_buf_new, lru_h_new, ssd_buf_new, ssd_h_new.astype(x.dtype))


def mem_kv(mem, wk, wv):
    Bsz, M, _ = mem.shape
    k = (mem @ wk).reshape(Bsz, M, MEM_HEADS, MEM_HEAD_DIM)
    v = (mem @ wv).reshape(Bsz, M, MEM_HEADS, MEM_HEAD_DIM)
    return k, v


def mem_attend(x, k, v, wq, wo):
    Bsz, T, _ = x.shape
    q = (x @ wq).reshape(Bsz, T, MEM_HEADS, MEM_HEAD_DIM)
    s = jnp.einsum('bthd,bmhd->bhtm', q, k).astype(jnp.float32) * (1.0 / math.sqrt(MEM_HEAD_DIM))
    p = jax.nn.softmax(s, axis=-1).astype(v.dtype)
    o = jnp.einsum('bhtm,bmhd->bthd', p, v).reshape(Bsz, T, D_MODEL)
    return o @ wo


def peer_route(xf, wq, subkeys):
    n = xf.shape[0]
    q = (xf @ wq).reshape(n, PEER_HEADS, 2, PEER_HALF)
    s = jnp.einsum('nhcd,hckd->nhck', q, subkeys).astype(jnp.float32)
    top_s, top_i = lax.top_k(s, PEER_TOPK)
    cand_s = top_s[:, :, 0, :, None] + top_s[:, :, 1, None, :]
    cand_i = top_i[:, :, 0, :, None] * PEER_NKEYS + top_i[:, :, 1, None, :]
    best_s, best = lax.top_k(cand_s.reshape(n, PEER_HEADS, PEER_TOPK * PEER_TOPK), PEER_TOPK)
    idx = jnp.take_along_axis(cand_i.reshape(n, PEER_HEADS, PEER_TOPK * PEER_TOPK), best, axis=-1)
    return idx, jax.nn.softmax(best_s, axis=-1)


def peer_ffn(x, wq, subkeys, u, v):
    Bsz, T, D = x.shape
    n = Bsz * T
    xf = x.reshape(n, D)
    idx, g = peer_route(xf, wq, subkeys)
    blk = min(PEER_BLOCK, n)
    nb = -(-n // blk)
    pad = nb * blk - n
    xp = jnp.pad(xf, ((0, pad), (0, 0))).reshape(nb, blk, D)
    ip = jnp.pad(idx, ((0, pad), (0, 0), (0, 0))).reshape(nb, blk, PEER_HEADS, PEER_TOPK)
    gp = jnp.pad(g, ((0, pad), (0, 0), (0, 0))).reshape(nb, blk, PEER_HEADS, PEER_TOPK)

    def block(args):
        xb, ib, gb = args
        act = jax.nn.gelu(jnp.einsum('nd,nhkd->nhk', xb, u[ib]), approximate=False)
        return jnp.einsum('nhk,nhkd->nd', (gb * act).astype(v.dtype), v[ib])

    out = lax.map(block, (xp, ip, gp)).reshape(nb * blk, D)[:n]
    return out.reshape(Bsz, T, D)


def trunk_layer(x, mem_k, mem_v, lru_buf, lru_h, ssd_buf, ssd_h, w):
    mix, states = parallel_mixer(x, lru_buf, lru_h, ssd_buf, ssd_h, w)
    x = layer_norm(DN_ALPHA * x + mix, w['ln1_g'], w['ln1_b'])
    x = layer_norm(DN_ALPHA * x + mem_attend(x, mem_k, mem_v, w['mem_wq'], w['mem_wo']),
                   w['ln2_g'], w['ln2_b'])
    x = layer_norm(DN_ALPHA * x + peer_ffn(x, w['peer_wq'], w['peer_subkeys'], w['peer_u'], w['peer_v']),
                   w['ln3_g'], w['ln3_b'])
    return x, states


def setup_inputs(seed: int = 0) -> dict:
    key = jax.random.key(seed)
    ks = iter(jax.random.split(key, 64))
    f32 = jnp.float32

    def nrm(shape, scale):
        return jax.random.normal(next(ks), shape, f32) * scale

    def gain(shape):
        return 1.0 + nrm(shape, 0.02)

    a0 = jax.random.uniform(next(ks), (DEPTH, D_LRU), f32, 0.9, 0.999)
    s0 = a0 ** (1.0 / LRU_C)
    lam = jnp.log(s0) - jnp.log1p(-s0)
    dt0 = jnp.exp(jax.random.uniform(next(ks), (DEPTH, SSD_HEADS), f32, math.log(1e-3), math.log(1e-1)))
    dt_bias = dt0 + jnp.log(-jnp.expm1(-dt0))
    A_log = jnp.log(jax.random.uniform(next(ks), (DEPTH, SSD_HEADS), f32, 1.0, 16.0))
    return {
        'x_prompt': nrm((BATCH, SEQ, D_MODEL), 1.0),
        'x_sample': nrm((DEC_BATCH, DEC_SEQ, D_MODEL), 1.0),
        'mem_prompt': nrm((BATCH, MEM_LEN, D_MODEL), 1.0),
        'state_lru_conv': nrm((DEPTH, DEC_BATCH, CONV_W - 1, D_LRU), 1.0),
        'state_lru_h': nrm((DEPTH, DEC_BATCH, D_LRU), 0.5),
        'state_ssd_conv': nrm((DEPTH, DEC_BATCH, CONV_W - 1, D_XBC), 1.0),
        'state_ssd_h': nrm((DEPTH, DEC_BATCH, SSD_HEADS, SSD_HEAD_DIM, SSD_STATE), 0.1),
        'cache_mem_k': nrm((DEPTH, DEC_BATCH, MEM_LEN, MEM_HEADS, MEM_HEAD_DIM), 1.0),
        'cache_mem_v': nrm((DEPTH, DEC_BATCH, MEM_LEN, MEM_HEADS, MEM_HEAD_DIM), 1.0),
        'w_in': nrm((DEPTH, D_MODEL, D_IN_PROJ), D_MODEL ** -0.5),
        'lru_conv_w': nrm((DEPTH, CONV_W, D_LRU), CONV_W ** -0.5),
        'lru_conv_b': nrm((DEPTH, D_LRU), 0.02),
        'lru_gate_a_w': nrm((DEPTH, LRU_BLOCKS, LRU_BLOCK, LRU_BLOCK), LRU_BLOCK ** -0.5),
        'lru_gate_a_b': nrm((DEPTH, D_LRU), 0.02),
        'lru_gate_x_w': nrm((DEPTH, LRU_BLOCKS, LRU_BLOCK, LRU_BLOCK), LRU_BLOCK ** -0.5),
        'lru_gate_x_b': nrm((DEPTH, D_LRU), 0.02),
        'lru_lambda': lam,
        'lru_norm_g': gain((DEPTH, D_LRU)),
        'ssd_conv_w': nrm((DEPTH, CONV_W, D_XBC), CONV_W ** -0.5),
        'ssd_conv_b': nrm((DEPTH, D_XBC), 0.02),
        'ssd_dt_bias': dt_bias,
        'ssd_A_log': A_log,
        'ssd_D': 1.0 + nrm((DEPTH, SSD_HEADS), 0.1),
        'ssd_norm_g': gain((DEPTH, D_SSD)),
        'w_out': nrm((DEPTH, D_MIX, D_MODEL), DN_BETA * D_MIX ** -0.5),
        'ln1_g': gain((DEPTH, D_MODEL)),
        'ln1_b': nrm((DEPTH, D_MODEL), 0.02),
        'mem_wq': nrm((DEPTH, D_MODEL, D_MODEL), D_MODEL ** -0.5),
        'mem_wk': nrm((DEPTH, D_MODEL, D_MODEL), D_MODEL ** -0.5),
        'mem_wv': nrm((DEPTH, D_MODEL, D_MODEL), D_MODEL ** -0.5),
        'mem_wo': nrm((DEPTH, D_MODEL, D_MODEL), DN_BETA * D_MODEL ** -0.5),
        'ln2_g': gain((DEPTH, D_MODEL)),
        'ln2_b': nrm((DEPTH, D_MODEL), 0.02),
        'peer_wq': nrm((DEPTH, D_MODEL, PEER_HEADS * PEER_DKEY), D_MODEL ** -0.5),
        'peer_subkeys': nrm((DEPTH, PEER_HEADS, 2, PEER_NKEYS, PEER_HALF), PEER_HALF ** -0.5),
        'peer_u': nrm((DEPTH, PEER_EXPERTS, D_MODEL), D_MODEL ** -0.5),
        'peer_v': nrm((DEPTH, PEER_EXPERTS, D_MODEL), DN_BETA * PEER_HEADS ** -0.5),
        'ln3_g': gain((DEPTH, D_MODEL)),
        'ln3_b': nrm((DEPTH, D_MODEL), 0.02),
    }


def reference(x_prompt, x_sample, mem_prompt, state_lru_conv, state_lru_h, state_ssd_conv, state_ssd_h,
              cache_mem_k, cache_mem_v, w_in, lru_conv_w, lru_conv_b, lru_gate_a_w, lru_gate_a_b,
              lru_gate_x_w, lru_gate_x_b, lru_lambda, lru_norm_g, ssd_conv_w, ssd_conv_b, ssd_dt_bias,
              ssd_A_log, ssd_D, ssd_norm_g, w_out, ln1_g, ln1_b, mem_wq, mem_wk, mem_wv, mem_wo,
              ln2_g, ln2_b, peer_wq, peer_subkeys, peer_u, peer_v, ln3_g, ln3_b):
    Bp = x_prompt.shape[0]
    dtype = x_prompt.dtype
    yp, ys = x_prompt, x_sample
    p_lc, p_lh, p_sc, p_sh, p_mk, p_mv = [], [], [], [], [], []
    s_lc, s_lh, s_sc, s_sh = [], [], [], []
    for l in range(DEPTH):
        w = dict(w_in=w_in[l], lru_conv_w=lru_conv_w[l], lru_conv_b=lru_conv_b[l],
                 lru_gate_a_w=lru_gate_a_w[l], lru_gate_a_b=lru_gate_a_b[l],
                 lru_gate_x_w=lru_gate_x_w[l], lru_gate_x_b=lru_gate_x_b[l],
                 lru_lambda=lru_lambda[l], lru_norm_g=lru_norm_g[l],
                 ssd_conv_w=ssd_conv_w[l], ssd_conv_b=ssd_conv_b[l], ssd_dt_bias=ssd_dt_bias[l],
                 ssd_A_log=ssd_A_log[l], ssd_D=ssd_D[l], ssd_norm_g=ssd_norm_g[l], w_out=w_out[l],
                 ln1_g=ln1_g[l], ln1_b=ln1_b[l], mem_wq=mem_wq[l], mem_wo=mem_wo[l],
                 ln2_g=ln2_g[l], ln2_b=ln2_b[l], peer_wq=peer_wq[l], peer_subkeys=peer_subkeys[l],
                 peer_u=peer_u[l], peer_v=peer_v[l], ln3_g=ln3_g[l], ln3_b=ln3_b[l])
        mk, mv = mem_kv(mem_prompt, mem_wk[l], mem_wv[l])
        yp, (lc, lh, sc, sh) = trunk_layer(
            yp, mk, mv,
            jnp.zeros((Bp, CONV_W - 1, D_LRU), dtype), jnp.zeros((Bp, D_LRU), dtype),
            jnp.zeros((Bp, CONV_W - 1, D_XBC), dtype),
            jnp.zeros((Bp, SSD_HEADS, SSD_HEAD_DIM, SSD_STATE), dtype), w)
        p_lc.append(lc); p_lh.append(lh); p_sc.append(sc); p_sh.append(sh)
        p_mk.append(mk); p_mv.append(mv)
        ys, (lc, lh, sc, sh) = trunk_layer(
            ys, cache_mem_k[l], cache_mem_v[l],
            state_lru_conv[l], state_lru_h[l], state_ssd_conv[l], state_ssd_h[l], w)
        s_lc.append(lc); s_lh.append(lh); s_sc.append(sc); s_sh.append(sh)
    return (yp, ys,
            jnp.stack(p_lc), jnp.stack(p_lh), jnp.stack(p_sc), jnp.stack(p_sh),
            jnp.stack(p_mk), jnp.stack(p_mv),
            jnp.stack(s_lc), jnp.stack(s_lh), jnp.stack(s_sc), jnp.stack(s_sh))
```

```python
import functools
import math

import jax
import jax.numpy as jnp
from jax import lax
from jax.experimental import pallas as pl
from jax.experimental.pallas import tpu as pltpu

F32 = jnp.float32
BF16 = jnp.bfloat16

LANES = 128
SUBLANES = 8
VMEM_LIMIT_BYTES = 56 * 1024 * 1024

CONV_W = 4
LRU_BLOCK = 128
LRU_C = 8.0
SSD_HEAD_DIM = 64
SSD_GROUPS = 4
SSD_STATE = 128
SSD_CHUNK = 128
MEM_HEADS = 4
PEER_HEADS = 8
PEER_NKEYS = 128
PEER_TOPK = 16
LN_EPS = 1e-5
RMS_EPS = 1e-6
NEG_INF = float("-inf")


def _params(*sem):
    return pltpu.CompilerParams(dimension_semantics=sem, vmem_limit_bytes=VMEM_LIMIT_BYTES)


def _dot(a, b):
    return jnp.dot(a, b, preferred_element_type=F32)


def _dot_nt(a, b):
    return lax.dot_general(a, b, (((1,), (1,)), ((), ())), preferred_element_type=F32)


def _dot_split3(m01, x):
    hi = x.astype(BF16)
    r1 = x - hi.astype(F32)
    mid = r1.astype(BF16)
    lo = (r1 - mid.astype(F32)).astype(BF16)
    return _dot(m01, hi) + _dot(m01, mid) + _dot(m01, lo)


def _sigmoid(x):
    return jax.nn.sigmoid(x)


def _silu(x):
    return x * _sigmoid(x)


def _gelu(x):
    return 0.5 * x * (1.0 + lax.erf(x * (1.0 / math.sqrt(2.0))))


def _softplus(x):
    return jnp.maximum(x, 0.0) + jnp.log1p(jnp.exp(-jnp.abs(x)))


def _expm1(z):
    p = 1.0 / 5040.0
    for c in (1.0 / 720.0, 1.0 / 120.0, 1.0 / 24.0, 1.0 / 6.0, 0.5, 1.0):
        p = p * z + c
    return jnp.where(z > -0.25, p * z, jnp.exp(z) - 1.0)


def _layer_norm(y, g, b):
    mu = jnp.mean(y, -1, keepdims=True)
    d = y - mu
    var = jnp.mean(d * d, -1, keepdims=True)
    return d * lax.rsqrt(var + LN_EPS) * g + b


def _rms_norm(v, g):
    return v * lax.rsqrt(jnp.mean(v * v, -1, keepdims=True) + RMS_EPS) * g


def _mm_kernel(a_ref, w_ref, o_ref, abf_ref):
    @pl.when(pl.program_id(1) == 0)
    def _():
        abf_ref[...] = a_ref[...].astype(BF16)

    o_ref[...] = _dot(abf_ref[...], w_ref[...].astype(BF16)).astype(o_ref.dtype)


def _mm(a, w, *, ncols=None, out_dtype=F32, tm, tn, name):
    m, k = a.shape
    n = ncols or w.shape[1]
    assert m % tm == 0 and n % tn == 0 and w.shape[0] == k
    return pl.pallas_call(
        _mm_kernel,
        grid=(m // tm, n // tn),
        in_specs=[pl.BlockSpec((tm, k), lambda i, j: (i, 0)),
                  pl.BlockSpec((k, tn), lambda i, j: (0, j))],
        out_specs=pl.BlockSpec((tm, tn), lambda i, j: (i, j)),
        out_shape=jax.ShapeDtypeStruct((m, n), out_dtype),
        scratch_shapes=[pltpu.VMEM((tm, k), BF16)],
        compiler_params=_params("parallel", "arbitrary"),
        name=name,
    )(a, w)


def _mm_ln_kernel(*refs, n_a, nk_each, alpha):
    a_refs = refs[:n_a]
    w_ref, res_ref, g_ref, b_ref, o_ref, acc_ref = refs[n_a:]
    k = pl.program_id(1)

    @pl.when(k == 0)
    def _():
        acc_ref[...] = jnp.zeros_like(acc_ref)

    for idx in range(n_a):
        @pl.when((k >= idx * nk_each) & (k < (idx + 1) * nk_each))
        def _(idx=idx):
            acc_ref[...] += _dot(a_refs[idx][...].astype(BF16), w_ref[...].astype(BF16))

    @pl.when(k == n_a * nk_each - 1)
    def _():
        o_ref[...] = _layer_norm(alpha * res_ref[...] + acc_ref[...], g_ref[...], b_ref[...])


def _mm_ln(a_list, w, res, g, b, *, alpha, tm, tk, name):
    m, n = res.shape
    ka = a_list[0].shape[1]
    n_a = len(a_list)
    assert all(a.shape == (m, ka) for a in a_list) and w.shape == (n_a * ka, n)
    assert m % tm == 0 and ka % tk == 0
    nk_each = ka // tk

    def a_map(idx):
        return lambda i, k: (i, jnp.clip(k - idx * nk_each, 0, nk_each - 1))

    return pl.pallas_call(
        functools.partial(_mm_ln_kernel, n_a=n_a, nk_each=nk_each, alpha=alpha),
        grid=(m // tm, n_a * nk_each),
        in_specs=[pl.BlockSpec((tm, tk), a_map(idx)) for idx in range(n_a)] + [
            pl.BlockSpec((tk, n), lambda i, k: (k, 0)),
            pl.BlockSpec((tm, n), lambda i, k: (i, 0)),
            pl.BlockSpec((1, n), lambda i, k: (0, 0)),
            pl.BlockSpec((1, n), lambda i, k: (0, 0))],
        out_specs=pl.BlockSpec((tm, n), lambda i, k: (i, 0)),
        out_shape=jax.ShapeDtypeStruct((m, n), F32),
        scratch_shapes=[pltpu.VMEM((tm, n), F32)],
        compiler_params=_params("parallel", "arbitrary"),
        name=name,
    )(*a_list, w, res, g.reshape(1, n), b.reshape(1, n))


def _lru_gate_block(xck, wa_ref, wx_ref, ba, bx, sp, kblk):
    xb = xck.astype(BF16)
    r = _sigmoid(_dot(xb, wa_ref[kblk].astype(BF16)) + ba)
    i = _sigmoid(_dot(xb, wx_ref[kblk].astype(BF16)) + bx)
    log_a = -LRU_C * r * sp
    a = jnp.exp(log_a)
    mult = jnp.sqrt(jnp.maximum(-_expm1(2.0 * log_a), 0.0))
    return a, mult * (i * xck)


def _lru_prompt_kernel(x_ref, gt_ref, cw_ref, cb_ref, wa_ref, ba_ref, wx_ref, bx_ref, lam_ref, ng_ref,
                       y_ref, hl_ref, xe_scr, a_scr, u_scr, h_scr):
    tt, c = x_ref.shape
    halo = SUBLANES

    @pl.when(pl.program_id(1) == 0)
    def _():
        xe_scr[0:halo, :] = jnp.zeros((halo, c), F32)
        h_scr[...] = jnp.zeros_like(h_scr)

    x = x_ref[...]
    xe_scr[halo:halo + tt, :] = x
    xc = cb_ref[...]
    for k in range(CONV_W - 1):
        off = halo - (CONV_W - 1) + k
        xc = xc + cw_ref[k:k + 1, :] * xe_scr[off:off + tt, :]
    xc = xc + cw_ref[CONV_W - 1:CONV_W, :] * x

    sp = _softplus(-lam_ref[...])
    for kblk in range(c // LRU_BLOCK):
        sl = slice(kblk * LRU_BLOCK, (kblk + 1) * LRU_BLOCK)
        a, u = _lru_gate_block(xc[:, sl], wa_ref, wx_ref, ba_ref[:, sl], bx_ref[:, sl], sp[:, sl], kblk)
        a_scr[:, sl] = a
        u_scr[:, sl] = u

    row = lax.broadcasted_iota(jnp.int32, (SUBLANES, c), 0)

    def scan_rows(i, h):
        r0 = pl.multiple_of(i * SUBLANES, SUBLANES)
        a = a_scr[pl.ds(r0, SUBLANES), :]
        u = u_scr[pl.ds(r0, SUBLANES), :]
        for s in (1, 2, 4):
            keep = row >= s
            a_prev = jnp.where(keep, pltpu.roll(a, s, 0), 1.0)
            u_prev = jnp.where(keep, pltpu.roll(u, s, 0), 0.0)
            u = a * u_prev + u
            a = a * a_prev
        hh = a * h + u
        u_scr[pl.ds(r0, SUBLANES), :] = hh
        return hh[SUBLANES - 1:SUBLANES, :]

    h_last = lax.fori_loop(0, tt // SUBLANES, scan_rows, h_scr[0:1, :])

    y_ref[...] = _rms_norm(u_scr[...] * _gelu(gt_ref[...]), ng_ref[...]).astype(y_ref.dtype)
    xe_scr[0:halo, :] = x[tt - halo:tt, :]
    h_scr[...] = jnp.broadcast_to(h_last, h_scr.shape)
    hl_ref[0] = h_last


def _vec(p):
    return p.reshape(1, -1)


def _lru_prompt(proj, bsz, seq, w, *, tt):
    c = w["lru_lambda"].shape[0]
    nblk = c // LRU_BLOCK
    nt = seq // tt
    assert seq % tt == 0 and tt % SUBLANES == 0 and tt >= SUBLANES
    row = lambda b, t: b * nt + t
    vec_spec = pl.BlockSpec((1, c), lambda b, t: (0, 0))
    gate_w_spec = pl.BlockSpec((nblk, LRU_BLOCK, LRU_BLOCK), lambda b, t: (0, 0, 0))
    return pl.pallas_call(
        _lru_prompt_kernel,
        grid=(bsz, nt),
        in_specs=[pl.BlockSpec((tt, c), lambda b, t: (row(b, t), 0)),
                  pl.BlockSpec((tt, c), lambda b, t: (row(b, t), 1)),
                  pl.BlockSpec((CONV_W, c), lambda b, t: (0, 0)), vec_spec,
                  gate_w_spec, vec_spec, gate_w_spec, vec_spec, vec_spec, vec_spec],
        out_specs=[pl.BlockSpec((tt, c), lambda b, t: (row(b, t), 0)),
                   pl.BlockSpec((1, 1, c), lambda b, t: (b, 0, 0))],
        out_shape=[jax.ShapeDtypeStruct((bsz * seq, c), BF16),
                   jax.ShapeDtypeStruct((bsz, 1, c), F32)],
        scratch_shapes=[pltpu.VMEM((SUBLANES + tt, c), F32), pltpu.VMEM((tt, c), F32),
                        pltpu.VMEM((tt, c), F32), pltpu.VMEM((SUBLANES, c), F32)],
        compiler_params=_params("parallel", "arbitrary"),
        name="lru_prompt",
    )(proj, proj, w["lru_conv_w"], _vec(w["lru_conv_b"]), w["lru_gate_a_w"], _vec(w["lru_gate_a_b"]),
      w["lru_gate_x_w"], _vec(w["lru_gate_x_b"]), _vec(w["lru_lambda"]), _vec(w["lru_norm_g"]))


def _lru_sample_kernel(x_ref, gt_ref, buf_ref, h0_ref, cw_ref, cb_ref, wa_ref, ba_ref, wx_ref, bx_ref,
                       lam_ref, ng_ref, y_ref, hn_ref):
    n, c = x_ref.shape
    xc = cb_ref[...]
    for k in range(CONV_W - 1):
        xc = xc + cw_ref[k:k + 1, :] * buf_ref[:, k * c:(k + 1) * c]
    xc = xc + cw_ref[CONV_W - 1:CONV_W, :] * x_ref[...]
    sp = _softplus(-lam_ref[...])
    for kblk in range(c // LRU_BLOCK):
        sl = slice(kblk * LRU_BLOCK, (kblk + 1) * LRU_BLOCK)
        a, u = _lru_gate_block(xc[:, sl], wa_ref, wx_ref, ba_ref[:, sl], bx_ref[:, sl], sp[:, sl], kblk)
        hn_ref[:, sl] = a * h0_ref[:, sl] + u
    y_ref[...] = _rms_norm(hn_ref[...] * _gelu(gt_ref[...]), ng_ref[...]).astype(y_ref.dtype)


def _lru_sample(proj, buf, h0, w):
    n = proj.shape[0]
    c = w["lru_lambda"].shape[0]
    nblk = c // LRU_BLOCK
    vec_spec = pl.BlockSpec((1, c), lambda i: (0, 0))
    gate_w_spec = pl.BlockSpec((nblk, LRU_BLOCK, LRU_BLOCK), lambda i: (0, 0, 0))
    full = pl.BlockSpec((n, c), lambda i: (0, 0))
    return pl.pallas_call(
        _lru_sample_kernel,
        grid=(1,),
        in_specs=[full, pl.BlockSpec((n, c), lambda i: (0, 1)),
                  pl.BlockSpec((n, (CONV_W - 1) * c), lambda i: (0, 0)), full,
                  pl.BlockSpec((CONV_W, c), lambda i: (0, 0)), vec_spec,
                  gate_w_spec, vec_spec, gate_w_spec, vec_spec, vec_spec, vec_spec],
        out_specs=[full, full],
        out_shape=[jax.ShapeDtypeStruct((n, c), BF16), jax.ShapeDtypeStruct((n, c), F32)],
        compiler_params=_params("arbitrary"),
        name="lru_sample",
    )(proj, proj, buf.reshape(n, (CONV_W - 1) * c), h0, w["lru_conv_w"], _vec(w["lru_conv_b"]),
      w["lru_gate_a_w"], _vec(w["lru_gate_a_b"]), w["lru_gate_x_w"], _vec(w["lru_gate_x_b"]),
      _vec(w["lru_lambda"]), _vec(w["lru_norm_g"]))


def _ssd_prompt_kernel(x_ref, z_ref, xbc_ref, wdt_ref, dtb_ref, alog_ref, cw_ref, cb_ref, dvec_ref, ng_ref,
                       y_ref, sl_ref, xe_scr, s_scr, y_scr):
    q, d_ssd = z_ref.shape
    halo = SUBLANES
    gw = SSD_GROUPS * SSD_STATE
    hpg = d_ssd // SSD_HEAD_DIM // SSD_GROUPS
    grows = hpg * SSD_HEAD_DIM

    @pl.when(pl.program_id(1) == 0)
    def _():
        xe_scr[0:halo, :] = jnp.zeros((halo, xe_scr.shape[1]), F32)
        s_scr[...] = jnp.zeros_like(s_scr)

    xbc = xbc_ref[...]
    xe_scr[halo:halo + q, :] = xbc
    conv = cb_ref[...]
    for k in range(CONV_W - 1):
        off = halo - (CONV_W - 1) + k
        conv = conv + cw_ref[k:k + 1, :] * xe_scr[off:off + q, :]
    act = _silu(conv + cw_ref[CONV_W - 1:CONV_W, :] * xbc)
    xs = act[:, :d_ssd]
    bm = act[:, d_ssd:d_ssd + gw].astype(BF16)
    cm = act[:, d_ssd + gw:].astype(BF16)

    dt = _softplus(_dot(x_ref[...].astype(BF16), wdt_ref[...].astype(BF16)) + dtb_ref[...])
    ag = dt * (-jnp.exp(alog_ref[...]))
    ri = lax.broadcasted_iota(jnp.int32, (q, q), 0)
    ci = lax.broadcasted_iota(jnp.int32, (q, q), 1)
    causal = ri >= ci
    a_cs = _dot_split3(causal.astype(F32).astype(BF16), ag)
    a_cs_t = a_cs.T
    dtd_t = (dt * jnp.exp(a_cs[q - 1:q, :] - a_cs)).T
    xs_t = xs.T

    for g in range(SSD_GROUPS):
        bg = bm[:, g * SSD_STATE:(g + 1) * SSD_STATE]
        cg = cm[:, g * SSD_STATE:(g + 1) * SSD_STATE]
        cb_g = _dot_nt(cg, bg)
        s_old = s_scr[g * grows:(g + 1) * grows, :]
        y_off = _dot_nt(cg, s_old.astype(BF16))
        for r in range(hpg):
            h = g * hpg + r
            hs = slice(h * SSD_HEAD_DIM, (h + 1) * SSD_HEAD_DIM)
            col = a_cs[:, h:h + 1]
            seg = jnp.exp(jnp.where(causal, col - a_cs_t[h:h + 1, :], NEG_INF))
            xh = xs[:, hs]
            xdt = (xh * dt[:, h:h + 1]).astype(BF16)
            y_scr[:, hs] = (_dot((cb_g * seg).astype(BF16), xdt)
                            + jnp.exp(col) * y_off[:, r * SSD_HEAD_DIM:(r + 1) * SSD_HEAD_DIM]
                            + dvec_ref[:, hs] * xh)
            lhs = (xs_t[hs, :] * dtd_t[h:h + 1, :]).astype(BF16)
            s_scr[hs, :] = (jnp.exp(a_cs[q - 1:q, h:h + 1]) * s_old[r * SSD_HEAD_DIM:(r + 1) * SSD_HEAD_DIM, :]
                            + _dot(lhs, bg))

    y_ref[...] = _rms_norm(y_scr[...] * _silu(z_ref[...]), ng_ref[...]).astype(y_ref.dtype)
    xe_scr[0:halo, :] = xbc[q - halo:q, :]

    @pl.when(pl.program_id(1) == pl.num_programs(1) - 1)
    def _():
        sl_ref[0] = s_scr[...]


def _pad_lanes(p2d):
    return jnp.pad(p2d, ((0, 0), (0, LANES - p2d.shape[1])))


def _ssd_head_params(w, w_in):
    d_ssd = w["ssd_norm_g"].shape[0]
    heads = d_ssd // SSD_HEAD_DIM
    assert heads <= LANES
    wdt = _pad_lanes(w_in[:, w_in.shape[1] - heads:])
    dtb = _pad_lanes(_vec(w["ssd_dt_bias"]))
    alog = _pad_lanes(_vec(w["ssd_A_log"]))
    dvec = _vec(jnp.repeat(w["ssd_D"], SSD_HEAD_DIM))
    return wdt, dtb, alog, dvec


def _ssd_prompt(x2d, proj, bsz, seq, w, w_in):
    d_model = x2d.shape[1]
    d_ssd = w["ssd_norm_g"].shape[0]
    d_xbc = w["ssd_conv_b"].shape[0]
    q = SSD_CHUNK
    nc = seq // q
    assert seq % q == 0 and (2 * d_ssd + d_ssd) % d_xbc == 0 and d_model == d_ssd
    z_blk = 2 * d_model // d_ssd
    xbc_blk = (2 * d_model + d_ssd) // d_xbc
    wdt, dtb, alog, dvec = _ssd_head_params(w, w_in)
    row = lambda b, c: b * nc + c
    lane_spec = pl.BlockSpec((1, LANES), lambda b, c: (0, 0))
    return pl.pallas_call(
        _ssd_prompt_kernel,
        grid=(bsz, nc),
        in_specs=[pl.BlockSpec((q, d_model), lambda b, c: (row(b, c), 0)),
                  pl.BlockSpec((q, d_ssd), lambda b, c: (row(b, c), z_blk)),
                  pl.BlockSpec((q, d_xbc), lambda b, c: (row(b, c), xbc_blk)),
                  pl.BlockSpec((d_model, LANES), lambda b, c: (0, 0)), lane_spec, lane_spec,
                  pl.BlockSpec((CONV_W, d_xbc), lambda b, c: (0, 0)),
                  pl.BlockSpec((1, d_xbc), lambda b, c: (0, 0)),
                  pl.BlockSpec((1, d_ssd), lambda b, c: (0, 0)),
                  pl.BlockSpec((1, d_ssd), lambda b, c: (0, 0))],
        out_specs=[pl.BlockSpec((q, d_ssd), lambda b, c: (row(b, c), 0)),
                   pl.BlockSpec((1, d_ssd, SSD_STATE), lambda b, c: (b, 0, 0))],
        out_shape=[jax.ShapeDtypeStruct((bsz * seq, d_ssd), BF16),
                   jax.ShapeDtypeStruct((bsz, d_ssd, SSD_STATE), F32)],
        scratch_shapes=[pltpu.VMEM((SUBLANES + q, d_xbc), F32), pltpu.VMEM((d_ssd, SSD_STATE), F32),
                        pltpu.VMEM((q, d_ssd), F32)],
        compiler_params=_params("parallel", "arbitrary"),
        name="ssd_prompt",
    )(x2d, proj, proj, wdt, dtb, alog, w["ssd_conv_w"], _vec(w["ssd_conv_b"]), dvec, _vec(w["ssd_norm_g"]))


def _ssd_sample_prep_kernel(x_ref, xbc_ref, buf_ref, wdt_ref, dtb_ref, alog_ref, cw_ref, cb_ref,
                            xs_ref, xdt_t_ref, b_ref, c_t_ref, dec_ref, xdt_scr):
    n, d_xbc = xbc_ref.shape
    d_ssd = xs_ref.shape[1]
    gw = SSD_GROUPS * SSD_STATE
    conv = cb_ref[...]
    for k in range(CONV_W - 1):
        conv = conv + cw_ref[k:k + 1, :] * buf_ref[:, k * d_xbc:(k + 1) * d_xbc]
    act = _silu(conv + cw_ref[CONV_W - 1:CONV_W, :] * xbc_ref[...])
    xs = act[:, :d_ssd]
    xs_ref[...] = xs
    b_ref[...] = act[:, d_ssd:d_ssd + gw]
    c_t_ref[...] = act[:, d_ssd + gw:].T
    dt = _softplus(_dot(x_ref[...].astype(BF16), wdt_ref[...].astype(BF16)) + dtb_ref[...])
    dec_ref[...] = jnp.exp(dt * (-jnp.exp(alog_ref[...])))
    for h in range(d_ssd // SSD_HEAD_DIM):
        hs = slice(h * SSD_HEAD_DIM, (h + 1) * SSD_HEAD_DIM)
        xdt_scr[:, hs] = xs[:, hs] * dt[:, h:h + 1]
    xdt_t_ref[...] = xdt_scr[...].T.astype(BF16)


def _ssd_sample_state_kernel(dec_ref, h0_ref, xdt_t_ref, b_ref, c_t_ref, xs_ref, z_ref, dvec_ref, ng_ref,
                             hn_ref, y_ref, yt_scr):
    bs, d_ssd, _ = h0_ref.shape
    n = xs_ref.shape[0]
    hpg = d_ssd // SSD_HEAD_DIM // SSD_GROUPS
    grows = hpg * SSD_HEAD_DIM
    step = pl.program_id(0)

    @pl.when(step == 0)
    def _():
        yt_scr[...] = jnp.zeros_like(yt_scr)

    seq_rows = lax.broadcasted_iota(jnp.int32, (n, SSD_STATE), 0)
    seq_lanes = lax.broadcasted_iota(jnp.int32, (SSD_STATE, n), 1)
    for j in range(bs):
        b = step * bs + j
        for g in range(SSD_GROUPS):
            gs = slice(g * SSD_STATE, (g + 1) * SSD_STATE)
            b_sel = jnp.where(seq_rows == b, b_ref[:, gs], 0.0).astype(BF16)
            c_sel = jnp.where(seq_lanes == b, c_t_ref[gs, :], 0.0).astype(BF16)
            outer = _dot(xdt_t_ref[g * grows:(g + 1) * grows, :], b_sel)
            for r in range(hpg):
                h = g * hpg + r
                hs = slice(h * SSD_HEAD_DIM, (h + 1) * SSD_HEAD_DIM)
                hn = dec_ref[b, h] * h0_ref[j, hs, :] + outer[r * SSD_HEAD_DIM:(r + 1) * SSD_HEAD_DIM, :]
                hn_ref[j, hs, :] = hn
                yt_scr[hs, :] += _dot(hn.astype(BF16), c_sel)

    @pl.when(step == pl.num_programs(0) - 1)
    def _():
        y = yt_scr[...].T + dvec_ref[...] * xs_ref[...]
        y_ref[...] = _rms_norm(y * _silu(z_ref[...]), ng_ref[...]).astype(y_ref.dtype)


def _ssd_sample(x2d, proj, buf, h0, w, w_in, *, bs):
    n, d_model = x2d.shape
    d_ssd = w["ssd_norm_g"].shape[0]
    d_xbc = w["ssd_conv_b"].shape[0]
    gw = SSD_GROUPS * SSD_STATE
    assert n == SSD_STATE == LANES and n % bs == 0
    z_blk = 2 * d_model // d_ssd
    xbc_blk = (2 * d_model + d_ssd) // d_xbc
    wdt, dtb, alog, dvec = _ssd_head_params(w, w_in)
    lane_spec = pl.BlockSpec((1, LANES), lambda i: (0, 0))
    xs, xdt_t, bmat, c_t, dec = pl.pallas_call(
        _ssd_sample_prep_kernel,
        grid=(1,),
        in_specs=[pl.BlockSpec((n, d_model), lambda i: (0, 0)),
                  pl.BlockSpec((n, d_xbc), lambda i: (0, xbc_blk)),
                  pl.BlockSpec((n, (CONV_W - 1) * d_xbc), lambda i: (0, 0)),
                  pl.BlockSpec((d_model, LANES), lambda i: (0, 0)), lane_spec, lane_spec,
                  pl.BlockSpec((CONV_W, d_xbc), lambda i: (0, 0)),
                  pl.BlockSpec((1, d_xbc), lambda i: (0, 0))],
        out_specs=[pl.BlockSpec((n, d_ssd), lambda i: (0, 0)),
                   pl.BlockSpec((d_ssd, n), lambda i: (0, 0)),
                   pl.BlockSpec((n, gw), lambda i: (0, 0)),
                   pl.BlockSpec((gw, n), lambda i: (0, 0)),
                   pl.BlockSpec((n, LANES), lambda i: (0, 0))],
        out_shape=[jax.ShapeDtypeStruct((n, d_ssd), F32), jax.ShapeDtypeStruct((d_ssd, n), BF16),
                   jax.ShapeDtypeStruct((n, gw), F32), jax.ShapeDtypeStruct((gw, n), F32),
                   jax.ShapeDtypeStruct((n, LANES), F32)],
        scratch_shapes=[pltpu.VMEM((n, d_ssd), F32)],
        compiler_params=_params("arbitrary"),
        name="ssd_sample_prep",
    )(x2d, proj, buf.reshape(n, (CONV_W - 1) * d_xbc), wdt, dtb, alog, w["ssd_conv_w"], _vec(w["ssd_conv_b"]))

    const = lambda shape: pl.BlockSpec(shape, lambda i: (0,) * len(shape))
    hn, y = pl.pallas_call(
        _ssd_sample_state_kernel,
        grid=(n // bs,),
        in_specs=[pl.BlockSpec(memory_space=pltpu.SMEM),
                  pl.BlockSpec((bs, d_ssd, SSD_STATE), lambda i: (i, 0, 0)),
                  const((d_ssd, n)), const((n, gw)), const((gw, n)), const((n, d_ssd)),
                  pl.BlockSpec((n, d_ssd), lambda i: (0, z_blk)), const((1, d_ssd)), const((1, d_ssd))],
        out_specs=[pl.BlockSpec((bs, d_ssd, SSD_STATE), lambda i: (i, 0, 0)), const((n, d_ssd))],
        out_shape=[jax.ShapeDtypeStruct((n, d_ssd, SSD_STATE), F32), jax.ShapeDtypeStruct((n, d_ssd), BF16)],
        scratch_shapes=[pltpu.VMEM((d_ssd, n), F32)],
        compiler_params=_params("arbitrary"),
        name="ssd_sample_state",
    )(dec, h0.reshape(n, d_ssd, SSD_STATE), xdt_t, bmat, c_t, xs, proj, dvec, _vec(w["ssd_norm_g"]))
    return y, hn


def _attn_prompt_kernel(q_ref, k_ref, v_ref, o_ref):
    d = q_ref.shape[1]
    hd = d // MEM_HEADS
    scale = 1.0 / math.sqrt(hd)
    for h in range(MEM_HEADS):
        sl = slice(h * hd, (h + 1) * hd)
        s = _dot_nt(q_ref[:, sl], k_ref[:, sl].astype(BF16)) * scale
        e = jnp.exp(s - jnp.max(s, -1, keepdims=True))
        p = e / jnp.sum(e, -1, keepdims=True)
        o_ref[:, sl] = _dot(p.astype(BF16), v_ref[:, sl].astype(BF16)).astype(o_ref.dtype)


def _attn_prompt(q, mk, mv, bsz, seq, mem_len, *, tq):
    d = q.shape[1]
    nt = seq // tq
    assert seq % tq == 0
    kv_spec = pl.BlockSpec((mem_len, d), lambda b, t: (b, 0))
    return pl.pallas_call(
        _attn_prompt_kernel,
        grid=(bsz, nt),
        in_specs=[pl.BlockSpec((tq, d), lambda b, t: (b * nt + t, 0)), kv_spec, kv_spec],
        out_specs=pl.BlockSpec((tq, d), lambda b, t: (b * nt + t, 0)),
        out_shape=jax.ShapeDtypeStruct((bsz * seq, d), BF16),
        compiler_params=_params("parallel", "arbitrary"),
        name="attn_prompt",
    )(q, mk, mv)


def _attn_sample_kernel(q_ref, k_ref, v_ref, o_ref):
    bs, _, d = q_ref.shape
    hd = d // MEM_HEADS
    scale = 1.0 / math.sqrt(hd)
    for j in range(bs):
        prod = k_ref[j] * q_ref[j]
        for h in range(MEM_HEADS):
            sl = slice(h * hd, (h + 1) * hd)
            s = jnp.sum(prod[:, sl], -1, keepdims=True) * scale
            e = jnp.exp(s - jnp.max(s, 0, keepdims=True))
            p = e / jnp.sum(e, 0, keepdims=True)
            o_ref[j, :, sl] = jnp.sum(p * v_ref[j, :, sl], 0, keepdims=True)


def _attn_sample(q, ck, cv, *, bs):
    n, d = q.shape
    mem_len = ck.shape[1]
    assert n % bs == 0
    kv_spec = pl.BlockSpec((bs, mem_len, d), lambda i: (i, 0, 0))
    q_spec = pl.BlockSpec((bs, 1, d), lambda i: (i, 0, 0))
    return pl.pallas_call(
        _attn_sample_kernel,
        grid=(n // bs,),
        in_specs=[q_spec, kv_spec, kv_spec],
        out_specs=q_spec,
        out_shape=jax.ShapeDtypeStruct((n, 1, d), F32),
        compiler_params=_params("parallel"),
        name="attn_sample",
    )(q.reshape(n, 1, d), ck.reshape(n, mem_len, d), cv.reshape(n, mem_len, d)).reshape(n, d)


def _top_values(s):
    vals, cnts = [], []
    work = s
    for _ in range(PEER_TOPK):
        m = jnp.max(work, 0, keepdims=True)
        eq = work == m
        vals.append(m)
        cnts.append(jnp.sum(eq.astype(F32), 0, keepdims=True))
        work = jnp.where(eq, NEG_INF, work)
    return vals, cnts


def _stack_rows(rows):
    ridx = lax.broadcasted_iota(jnp.int32, (len(rows), rows[0].shape[1]), 0)
    out = jnp.broadcast_to(rows[0], ridx.shape)
    for i in range(1, len(rows)):
        out = jnp.where(ridx == i, rows[i], out)
    return out


def _peer_route_kernel(q_ref, sk_ref, s1_ref, c1_ref, s2_ref, e2_ref, tau_ref):
    half = sk_ref.shape[3]
    for h in range(PEER_HEADS):
        s = [_dot_nt(sk_ref[h, c].astype(BF16), q_ref[:, (2 * h + c) * half:(2 * h + c + 1) * half])
             for c in range(2)]
        (va, ca), (vb, cb) = _top_values(s[0]), _top_values(s[1])
        vb_all, cb_all = _stack_rows(vb), _stack_rows(cb)
        cand = [va[k] + vb_all for k in range(PEER_TOPK)]
        mult = [ca[k] * cb_all for k in range(PEER_TOPK)]
        work = cand
        seen = jnp.zeros_like(va[0])
        top = va[0] + vb[0]
        tau = top
        for _ in range(PEER_TOPK):
            m = jnp.max(functools.reduce(jnp.maximum, work), 0, keepdims=True)
            eq = [blk == m for blk in work]
            tau = jnp.where(seen < PEER_TOPK, m, tau)
            hit = functools.reduce(jnp.add, [jnp.where(q, c, 0.0) for q, c in zip(eq, mult)])
            seen = seen + jnp.sum(hit, 0, keepdims=True)
            work = [jnp.where(q, NEG_INF, blk) for q, blk in zip(eq, work)]
        zs = functools.reduce(jnp.add, [jnp.where(blk >= tau, c * jnp.exp(blk - top), 0.0)
                                        for blk, c in zip(cand, mult)])
        z = jnp.sum(zs, 0, keepdims=True)
        s1_ref[h] = s[0]
        s2_ref[h] = s[1]
        c1_ref[h] = jnp.exp(s[0] - va[0]) / z
        e2_ref[h] = jnp.exp(s[1] - vb[0])
        tau_ref[h:h + 1, :] = tau


def _peer_route(q, subkeys, *, tn):
    n = q.shape[0]
    heads, _, nkeys, half = subkeys.shape
    assert heads == PEER_HEADS and nkeys == PEER_NKEYS and n % tn == 0
    score_spec = pl.BlockSpec((heads, nkeys, tn), lambda i: (0, 0, i))
    score_shape = jax.ShapeDtypeStruct((heads, nkeys, n), F32)
    return pl.pallas_call(
        _peer_route_kernel,
        grid=(n // tn,),
        in_specs=[pl.BlockSpec((tn, q.shape[1]), lambda i: (i, 0)),
                  pl.BlockSpec(subkeys.shape, lambda i: (0, 0, 0, 0))],
        out_specs=[score_spec] * 4 + [pl.BlockSpec((heads, tn), lambda i: (0, i))],
        out_shape=[score_shape] * 4 + [jax.ShapeDtypeStruct((heads, n), F32)],
        compiler_params=_params("parallel"),
        name="peer_route",
    )(q, subkeys)


def _peer_main_kernel(x_ref, s1_ref, c1_ref, s2_ref, e2_ref, tau_ref, u_ref, v_ref, g_ref, b_ref,
                      o_ref, xbf_scr, acc_scr, m_scr, *, alpha):
    et = u_ref.shape[0]
    e = pl.program_id(1)

    @pl.when(e == 0)
    def _():
        xbf_scr[...] = x_ref[...].astype(BF16)
        acc_scr[...] = jnp.zeros_like(acc_scr)

    act = _dot_nt(xbf_scr[...], u_ref[...].astype(BF16))
    for ii in range(et // PEER_NKEYS):
        i1 = e * (et // PEER_NKEYS) + ii
        w_t = None
        for h in range(PEER_HEADS):
            hit = s2_ref[h] + s1_ref[h, pl.ds(i1, 1), :] >= tau_ref[h:h + 1, :]
            term = jnp.where(hit, e2_ref[h] * c1_ref[h, pl.ds(i1, 1), :], 0.0)
            w_t = term if w_t is None else w_t + term
        sl = slice(ii * PEER_NKEYS, (ii + 1) * PEER_NKEYS)
        m_scr[:, sl] = (_gelu(act[:, sl]) * w_t.T).astype(BF16)
    acc_scr[...] += _dot(m_scr[...], v_ref[...].astype(BF16))

    @pl.when(e == pl.num_programs(1) - 1)
    def _():
        o_ref[...] = _layer_norm(alpha * x_ref[...] + acc_scr[...], g_ref[...], b_ref[...])


def _peer_main(x, route, u, v, g, b, *, alpha, tn, et):
    n, d = x.shape
    s1, c1, s2, e2, tau = route
    n_exp = u.shape[0]
    assert n_exp == PEER_NKEYS * PEER_NKEYS and n_exp % et == 0 and et % PEER_NKEYS == 0 and n % tn == 0
    once = pl.Buffered(1)
    score_spec = pl.BlockSpec((PEER_HEADS, PEER_NKEYS, tn), lambda i, e: (0, 0, i), pipeline_mode=once)
    return pl.pallas_call(
        functools.partial(_peer_main_kernel, alpha=alpha),
        grid=(n // tn, n_exp // et),
        in_specs=[pl.BlockSpec((tn, d), lambda i, e: (i, 0), pipeline_mode=once),
                  score_spec, score_spec, score_spec, score_spec,
                  pl.BlockSpec((PEER_HEADS, tn), lambda i, e: (0, i), pipeline_mode=once),
                  pl.BlockSpec((et, d), lambda i, e: (e, 0)),
                  pl.BlockSpec((et, d), lambda i, e: (e, 0)),
                  pl.BlockSpec((1, d), lambda i, e: (0, 0)),
                  pl.BlockSpec((1, d), lambda i, e: (0, 0))],
        out_specs=pl.BlockSpec((tn, d), lambda i, e: (i, 0)),
        out_shape=jax.ShapeDtypeStruct((n, d), F32),
        scratch_shapes=[pltpu.VMEM((tn, d), BF16), pltpu.VMEM((tn, d), F32), pltpu.VMEM((tn, et), BF16)],
        compiler_params=_params("parallel", "arbitrary"),
        name="peer_main",
    )(x, s1, c1, s2, e2, tau, u, v, _vec(g), _vec(b))


def _token_stages(x1_in, attn_fn, w, alpha, *, tm, tn_peer):
    d = x1_in.shape[1]
    q = _mm(x1_in, w["mem_wq"], out_dtype=attn_fn.q_dtype, tm=tm, tn=512, name="mem_q")
    o = attn_fn(q)
    x2 = _mm_ln([o], w["mem_wo"], x1_in, w["ln2_g"], w["ln2_b"], alpha=alpha, tm=min(tm, 512), tk=512,
                name="mem_out_ln")
    pq = _mm(x2, w["peer_wq"], out_dtype=BF16, tm=tm, tn=512, name="peer_q")
    route = _peer_route(pq, w["peer_subkeys"], tn=tn_peer)
    return _peer_main(x2, route, w["peer_u"], w["peer_v"], w["ln3_g"], w["ln3_b"], alpha=alpha,
                      tn=tn_peer, et=512)


class _AttnFn:
    def __init__(self, fn, q_dtype):
        self.fn, self.q_dtype = fn, q_dtype

    def __call__(self, q):
        return self.fn(q)


def _prompt_layer(x, mem, w, alpha):
    bsz, seq, d = x.shape
    x2d = x.reshape(bsz * seq, d)
    d_lru = w["lru_lambda"].shape[0]
    d_ssd = w["ssd_norm_g"].shape[0]
    d_xbc = w["ssd_conv_b"].shape[0]
    n_proj = 2 * d_lru + d_ssd + d_xbc
    assert seq >= CONV_W - 1
    proj = _mm(x2d, w["w_in"], ncols=n_proj, tm=1024, tn=512, name="in_proj")
    y_lru, lru_h = _lru_prompt(proj, bsz, seq, w, tt=256)
    y_ssd, ssd_h = _ssd_prompt(x2d, proj, bsz, seq, w, w["w_in"])
    tail = proj.reshape(bsz, seq, n_proj)[:, seq - (CONV_W - 1):]
    lru_conv = tail[:, :, :d_lru]
    ssd_conv = tail[:, :, n_proj - d_xbc:]
    x1 = _mm_ln([y_lru, y_ssd], w["w_out"], x2d, w["ln1_g"], w["ln1_b"], alpha=alpha, tm=512, tk=512,
                name="mix_out_ln")
    mem_len = mem.shape[1]
    mem2d = mem.reshape(bsz * mem_len, d)
    mk = _mm(mem2d, w["mem_wk"], tm=512, tn=512, name="mem_k")
    mv = _mm(mem2d, w["mem_wv"], tm=512, tn=512, name="mem_v")
    attn = _AttnFn(lambda q: _attn_prompt(q, mk, mv, bsz, seq, mem_len, tq=512), BF16)
    y = _token_stages(x1, attn, w, alpha, tm=1024, tn_peer=512)
    hd = d // MEM_HEADS
    states = (lru_conv, lru_h.reshape(bsz, d_lru), ssd_conv,
              ssd_h.reshape(bsz, d_ssd // SSD_HEAD_DIM, SSD_HEAD_DIM, SSD_STATE),
              mk.reshape(bsz, mem_len, MEM_HEADS, hd), mv.reshape(bsz, mem_len, MEM_HEADS, hd))
    return y.reshape(bsz, seq, d), states


def _sample_layer(x, ck, cv, lru_buf, lru_h, ssd_buf, ssd_h, w, alpha):
    n, seq, d = x.shape
    assert seq == 1
    x2d = x.reshape(n, d)
    d_lru = w["lru_lambda"].shape[0]
    d_ssd = w["ssd_norm_g"].shape[0]
    d_xbc = w["ssd_conv_b"].shape[0]
    n_proj = 2 * d_lru + d_ssd + d_xbc
    proj = _mm(x2d, w["w_in"], ncols=n_proj, tm=n, tn=512, name="in_proj_s")
    y_lru, lru_hn = _lru_sample(proj, lru_buf, lru_h, w)
    y_ssd, ssd_hn = _ssd_sample(x2d, proj, ssd_buf, ssd_h, w, w["w_in"], bs=4)
    lru_conv = jnp.concatenate([lru_buf[:, 1:], proj[:, None, :d_lru]], 1)
    ssd_conv = jnp.concatenate([ssd_buf[:, 1:], proj[:, None, n_proj - d_xbc:]], 1)
    x1 = _mm_ln([y_lru, y_ssd], w["w_out"], x2d, w["ln1_g"], w["ln1_b"], alpha=alpha, tm=n, tk=512,
                name="mix_out_ln_s")
    attn = _AttnFn(lambda q: _attn_sample(q, ck, cv, bs=2), F32)
    y = _token_stages(x1, attn, w, alpha, tm=n, tn_peer=n)
    states = (lru_conv, lru_hn, ssd_conv, ssd_hn.reshape(ssd_h.shape))
    return y.reshape(n, 1, d), states


def kernel(x_prompt, x_sample, mem_prompt, state_lru_conv, state_lru_h, state_ssd_conv, state_ssd_h,
           cache_mem_k, cache_mem_v, w_in, lru_conv_w, lru_conv_b, lru_gate_a_w, lru_gate_a_b,
           lru_gate_x_w, lru_gate_x_b, lru_lambda, lru_norm_g, ssd_conv_w, ssd_conv_b, ssd_dt_bias,
           ssd_A_log, ssd_D, ssd_norm_g, w_out, ln1_g, ln1_b, mem_wq, mem_wk, mem_wv, mem_wo,
           ln2_g, ln2_b, peer_wq, peer_subkeys, peer_u, peer_v, ln3_g, ln3_b):
    depth = w_in.shape[0]
    alpha = (2.0 * depth) ** 0.25
    weights = dict(w_in=w_in, lru_conv_w=lru_conv_w, lru_conv_b=lru_conv_b, lru_gate_a_w=lru_gate_a_w,
                   lru_gate_a_b=lru_gate_a_b, lru_gate_x_w=lru_gate_x_w, lru_gate_x_b=lru_gate_x_b,
                   lru_lambda=lru_lambda, lru_norm_g=lru_norm_g, ssd_conv_w=ssd_conv_w,
                   ssd_conv_b=ssd_conv_b, ssd_dt_bias=ssd_dt_bias, ssd_A_log=ssd_A_log, ssd_D=ssd_D,
                   ssd_norm_g=ssd_norm_g, w_out=w_out, ln1_g=ln1_g, ln1_b=ln1_b, mem_wq=mem_wq,
                   mem_wk=mem_wk, mem_wv=mem_wv, mem_wo=mem_wo, ln2_g=ln2_g, ln2_b=ln2_b,
                   peer_wq=peer_wq, peer_subkeys=peer_subkeys, peer_u=peer_u, peer_v=peer_v,
                   ln3_g=ln3_g, ln3_b=ln3_b)
    yp, ys = x_prompt, x_sample
    p_states, s_states = [], []
    for l in range(depth):
        w = {name: p[l] for name, p in weights.items()}
        yp, st = _prompt_layer(yp, mem_prompt, w, alpha)
        p_states.append(st)
        ys, st = _sample_layer(ys, cache_mem_k[l], cache_mem_v[l], state_lru_conv[l], state_lru_h[l],
                               state_ssd_conv[l], state_ssd_h[l], w, alpha)
        s_states.append(st)
    stack = lambda states, i: jnp.stack([st[i] for st in states])
    return (yp, ys) + tuple(stack(p_states, i) for i in range(6)) + tuple(stack(s_states, i) for i in range(4))
```

```python
import functools
import math

import jax
import jax.numpy as jnp
from jax import lax
from jax.experimental import pallas as pl
from jax.experimental.pallas import tpu as pltpu

F32 = jnp.float32
BF16 = jnp.bfloat16

LANES = 128
SUBLANES = 8
VMEM_LIMIT_BYTES = 56 * 1024 * 1024
CAST_ROWS = 1024

CONV_W = 4
LRU_BLOCK = 128
LRU_C = 8.0
SSD_HEAD_DIM = 64
SSD_GROUPS = 4
SSD_STATE = 128
SSD_CHUNK = 128
MEM_HEADS = 4
PEER_HEADS = 8
PEER_NKEYS = 128
PEER_TOPK = 16
LN_EPS = 1e-5
RMS_EPS = 1e-6
NEG_INF = float("-inf")


def _params(*sem):
    return pltpu.CompilerParams(dimension_semantics=sem, vmem_limit_bytes=VMEM_LIMIT_BYTES)


def _dot(a, b):
    return jnp.dot(a, b, preferred_element_type=F32)


def _dot_nt(a, b):
    return lax.dot_general(a, b, (((1,), (1,)), ((), ())), preferred_element_type=F32)


def _dot_split3(m01, x):
    hi = x.astype(BF16)
    r1 = x - hi.astype(F32)
    mid = r1.astype(BF16)
    lo = (r1 - mid.astype(F32)).astype(BF16)
    return _dot(m01, hi) + _dot(m01, mid) + _dot(m01, lo)


def _sigmoid(x):
    return jax.nn.sigmoid(x)


def _silu(x):
    return x * _sigmoid(x)


def _gelu(x):
    return 0.5 * x * (1.0 + lax.erf(x * (1.0 / math.sqrt(2.0))))


def _softplus(x):
    return jnp.maximum(x, 0.0) + jnp.log1p(jnp.exp(-jnp.abs(x)))


def _layer_norm(y, g, b):
    mu = jnp.mean(y, -1, keepdims=True)
    d = y - mu
    var = jnp.mean(d * d, -1, keepdims=True)
    return d * lax.rsqrt(var + LN_EPS) * g + b


def _rms_norm(v, g):
    return v * lax.rsqrt(jnp.mean(v * v, -1, keepdims=True) + RMS_EPS) * g


def _mm_kernel(a_ref, w_ref, o_ref, abf_ref):
    @pl.when(pl.program_id(1) == 0)
    def _():
        abf_ref[...] = a_ref[...].astype(BF16)

    o_ref[...] = _dot(abf_ref[...], w_ref[...].astype(BF16)).astype(o_ref.dtype)


def _mm(a, w, layer, *, ncols=None, out_dtype=F32, tm, tn, name):
    m, k = a.shape
    n = ncols or w.shape[2]
    assert m % tm == 0 and n % tn == 0 and w.shape[1] == k
    return pl.pallas_call(
        _mm_kernel,
        grid=(m // tm, n // tn),
        in_specs=[pl.BlockSpec((tm, k), lambda i, j: (i, 0), pipeline_mode=pl.Buffered(1)),
                  pl.BlockSpec((None, k, tn), lambda i, j: (layer, 0, j))],
        out_specs=pl.BlockSpec((tm, tn), lambda i, j: (i, j)),
        out_shape=jax.ShapeDtypeStruct((m, n), out_dtype),
        scratch_shapes=[pltpu.VMEM((tm, k), BF16)],
        compiler_params=_params("parallel", "arbitrary"),
        name=name,
    )(a, w)


def _mm_ln_kernel(*refs, n_a, nk_each, alpha):
    a_refs = refs[:n_a]
    w_ref, res_ref, g_ref, b_ref, o_ref = refs[n_a:]
    k = pl.program_id(1)

    @pl.when(k == 0)
    def _():
        o_ref[...] = jnp.zeros_like(o_ref)

    for idx in range(n_a):
        @pl.when((k >= idx * nk_each) & (k < (idx + 1) * nk_each))
        def _(idx=idx):
            o_ref[...] += _dot(a_refs[idx][...].astype(BF16), w_ref[...].astype(BF16))

    @pl.when(k == n_a * nk_each - 1)
    def _():
        o_ref[...] = _layer_norm(alpha * res_ref[...] + o_ref[...], g_ref[...], b_ref[...])


def _mm_ln(a_list, w, layer, res, g, b, *, alpha, tm, tk, name):
    m, n = res.shape
    ka = a_list[0].shape[1]
    n_a = len(a_list)
    assert all(a.shape == (m, ka) for a in a_list) and w.shape[1:] == (n_a * ka, n)
    assert m % tm == 0 and ka % tk == 0
    nk_each = ka // tk

    def a_map(idx):
        return lambda i, k: (i, jnp.clip(k - idx * nk_each, 0, nk_each - 1))

    return pl.pallas_call(
        functools.partial(_mm_ln_kernel, n_a=n_a, nk_each=nk_each, alpha=alpha),
        grid=(m // tm, n_a * nk_each),
        in_specs=[pl.BlockSpec((tm, tk), a_map(idx)) for idx in range(n_a)] + [
            pl.BlockSpec((None, tk, n), lambda i, k: (layer, k, 0)),
            pl.BlockSpec((tm, n), lambda i, k: (i, 0)),
            pl.BlockSpec((1, n), lambda i, k: (0, 0)),
            pl.BlockSpec((1, n), lambda i, k: (0, 0))],
        out_specs=pl.BlockSpec((tm, n), lambda i, k: (i, 0)),
        out_shape=jax.ShapeDtypeStruct((m, n), F32),
        compiler_params=_params("parallel", "arbitrary"),
        name=name,
    )(*a_list, w, res, g.reshape(1, n), b.reshape(1, n))


def _lru_gate_block(xck, wa_ref, wx_ref, ba, bx, sp, kblk):
    xb = xck.astype(BF16)
    r = _sigmoid(_dot(xb, wa_ref[kblk].astype(BF16)) + ba)
    i = _sigmoid(_dot(xb, wx_ref[kblk].astype(BF16)) + bx)
    log_a = -LRU_C * r * sp
    a = jnp.exp(log_a)
    mult = jnp.sqrt(jnp.maximum(1.0 - a * a, 0.0))
    return a, mult * (i * xck)


def _lru_prompt_kernel(x_ref, gt_ref, cw_ref, cb_ref, wa_ref, ba_ref, wx_ref, bx_ref, lam_ref, ng_ref,
                       y_ref, hl_ref, xe_scr, a_scr, u_scr, h_scr):
    tt, c = x_ref.shape
    halo = SUBLANES

    @pl.when(pl.program_id(1) == 0)
    def _():
        xe_scr[0:halo, :] = jnp.zeros((halo, c), F32)
        h_scr[...] = jnp.zeros_like(h_scr)

    x = x_ref[...]
    xe_scr[halo:halo + tt, :] = x
    xc = cb_ref[...]
    for k in range(CONV_W - 1):
        off = halo - (CONV_W - 1) + k
        xc = xc + cw_ref[k:k + 1, :] * xe_scr[off:off + tt, :]
    xc = xc + cw_ref[CONV_W - 1:CONV_W, :] * x

    sp = _softplus(-lam_ref[...])
    for kblk in range(c // LRU_BLOCK):
        sl = slice(kblk * LRU_BLOCK, (kblk + 1) * LRU_BLOCK)
        a, u = _lru_gate_block(xc[:, sl], wa_ref, wx_ref, ba_ref[:, sl], bx_ref[:, sl], sp[:, sl], kblk)
        a_scr[:, sl] = a
        u_scr[:, sl] = u

    row = lax.broadcasted_iota(jnp.int32, (SUBLANES, c), 0)

    def scan_rows(i, h):
        r0 = pl.multiple_of(i * SUBLANES, SUBLANES)
        a = a_scr[pl.ds(r0, SUBLANES), :]
        u = u_scr[pl.ds(r0, SUBLANES), :]
        for s in (1, 2, 4):
            keep = row >= s
            a_prev = jnp.where(keep, pltpu.roll(a, s, 0), 1.0)
            u_prev = jnp.where(keep, pltpu.roll(u, s, 0), 0.0)
            u = a * u_prev + u
            a = a * a_prev
        hh = a * h + u
        u_scr[pl.ds(r0, SUBLANES), :] = hh
        return hh[SUBLANES - 1:SUBLANES, :]

    h_last = lax.fori_loop(0, tt // SUBLANES, scan_rows, h_scr[0:1, :])

    y_ref[...] = _rms_norm(u_scr[...] * _gelu(gt_ref[...]), ng_ref[...]).astype(y_ref.dtype)
    xe_scr[0:halo, :] = x[tt - halo:tt, :]
    h_scr[...] = jnp.broadcast_to(h_last, h_scr.shape)
    hl_ref[0] = h_last


def _vec(p):
    return p.reshape(1, -1)


def _lru_prompt(proj, bsz, seq, w, layer, *, tt):
    c = w["lru_lambda"].shape[0]
    nblk = c // LRU_BLOCK
    nt = seq // tt
    assert seq % tt == 0 and tt % SUBLANES == 0 and tt >= SUBLANES
    row = lambda b, t: b * nt + t
    vec_spec = pl.BlockSpec((1, c), lambda b, t: (0, 0))
    gate_w_spec = pl.BlockSpec((None, nblk, LRU_BLOCK, LRU_BLOCK), lambda b, t: (layer, 0, 0, 0))
    return pl.pallas_call(
        _lru_prompt_kernel,
        grid=(bsz, nt),
        in_specs=[pl.BlockSpec((tt, c), lambda b, t: (row(b, t), 0)),
                  pl.BlockSpec((tt, c), lambda b, t: (row(b, t), 1)),
                  pl.BlockSpec((CONV_W, c), lambda b, t: (0, 0)), vec_spec,
                  gate_w_spec, vec_spec, gate_w_spec, vec_spec, vec_spec, vec_spec],
        out_specs=[pl.BlockSpec((tt, c), lambda b, t: (row(b, t), 0)),
                   pl.BlockSpec((1, 1, c), lambda b, t: (b, 0, 0))],
        out_shape=[jax.ShapeDtypeStruct((bsz * seq, c), BF16),
                   jax.ShapeDtypeStruct((bsz, 1, c), F32)],
        scratch_shapes=[pltpu.VMEM((SUBLANES + tt, c), F32), pltpu.VMEM((tt, c), F32),
                        pltpu.VMEM((tt, c), F32), pltpu.VMEM((SUBLANES, c), F32)],
        compiler_params=_params("parallel", "arbitrary"),
        name="lru_prompt",
    )(proj, proj, w["lru_conv_w"], _vec(w["lru_conv_b"]), w["lru_gate_a_w"], _vec(w["lru_gate_a_b"]),
      w["lru_gate_x_w"], _vec(w["lru_gate_x_b"]), _vec(w["lru_lambda"]), _vec(w["lru_norm_g"]))


def _lru_sample_kernel(x_ref, gt_ref, buf_ref, h0_ref, cw_ref, cb_ref, wa_ref, ba_ref, wx_ref, bx_ref,
                       lam_ref, ng_ref, y_ref, hn_ref):
    n, c = x_ref.shape
    xc = cb_ref[...]
    for k in range(CONV_W - 1):
        xc = xc + cw_ref[k:k + 1, :] * buf_ref[:, k * c:(k + 1) * c]
    xc = xc + cw_ref[CONV_W - 1:CONV_W, :] * x_ref[...]
    sp = _softplus(-lam_ref[...])
    for kblk in range(c // LRU_BLOCK):
        sl = slice(kblk * LRU_BLOCK, (kblk + 1) * LRU_BLOCK)
        a, u = _lru_gate_block(xc[:, sl], wa_ref, wx_ref, ba_ref[:, sl], bx_ref[:, sl], sp[:, sl], kblk)
        hn_ref[:, sl] = a * h0_ref[:, sl] + u
    y_ref[...] = _rms_norm(hn_ref[...] * _gelu(gt_ref[...]), ng_ref[...]).astype(y_ref.dtype)


def _lru_sample(proj, buf, h0, w, layer):
    n = proj.shape[0]
    c = w["lru_lambda"].shape[0]
    nblk = c // LRU_BLOCK
    vec_spec = pl.BlockSpec((1, c), lambda i: (0, 0))
    gate_w_spec = pl.BlockSpec((None, nblk, LRU_BLOCK, LRU_BLOCK), lambda i: (layer, 0, 0, 0))
    full = pl.BlockSpec((n, c), lambda i: (0, 0))
    return pl.pallas_call(
        _lru_sample_kernel,
        grid=(1,),
        in_specs=[full, pl.BlockSpec((n, c), lambda i: (0, 1)),
                  pl.BlockSpec((n, (CONV_W - 1) * c), lambda i: (0, 0)), full,
                  pl.BlockSpec((CONV_W, c), lambda i: (0, 0)), vec_spec,
                  gate_w_spec, vec_spec, gate_w_spec, vec_spec, vec_spec, vec_spec],
        out_specs=[full, full],
        out_shape=[jax.ShapeDtypeStruct((n, c), BF16), jax.ShapeDtypeStruct((n, c), F32)],
        compiler_params=_params("arbitrary"),
        name="lru_sample",
    )(proj, proj, buf.reshape(n, (CONV_W - 1) * c), h0, w["lru_conv_w"], _vec(w["lru_conv_b"]),
      w["lru_gate_a_w"], _vec(w["lru_gate_a_b"]), w["lru_gate_x_w"], _vec(w["lru_gate_x_b"]),
      _vec(w["lru_lambda"]), _vec(w["lru_norm_g"]))


def _ssd_prompt_kernel(x_ref, z_ref, xbc_ref, wdt_ref, dtb_ref, alog_ref, cw_ref, cb_ref, dvec_ref, ng_ref,
                       y_ref, sl_ref, xe_scr, s_scr, y_scr):
    q, d_ssd = z_ref.shape
    halo = SUBLANES
    gw = SSD_GROUPS * SSD_STATE
    hpg = d_ssd // SSD_HEAD_DIM // SSD_GROUPS
    grows = hpg * SSD_HEAD_DIM

    @pl.when(pl.program_id(1) == 0)
    def _():
        xe_scr[0:halo, :] = jnp.zeros((halo, xe_scr.shape[1]), F32)
        s_scr[...] = jnp.zeros_like(s_scr)

    xbc = xbc_ref[...]
    xe_scr[halo:halo + q, :] = xbc
    conv = cb_ref[...]
    for k in range(CONV_W - 1):
        off = halo - (CONV_W - 1) + k
        conv = conv + cw_ref[k:k + 1, :] * xe_scr[off:off + q, :]
    act = _silu(conv + cw_ref[CONV_W - 1:CONV_W, :] * xbc)
    xs = act[:, :d_ssd]
    bm = act[:, d_ssd:d_ssd + gw].astype(BF16)
    cm = act[:, d_ssd + gw:].astype(BF16)

    dt = _softplus(_dot(x_ref[...].astype(BF16), wdt_ref[...].astype(BF16)) + dtb_ref[...])
    ag = dt * (-jnp.exp(alog_ref[...]))
    ri = lax.broadcasted_iota(jnp.int32, (q, q), 0)
    ci = lax.broadcasted_iota(jnp.int32, (q, q), 1)
    causal = ri >= ci
    a_cs = _dot_split3(causal.astype(F32).astype(BF16), ag)
    a_cs_t = a_cs.T
    dtd_t = (dt * jnp.exp(a_cs[q - 1:q, :] - a_cs)).T
    xs_t = xs.T

    for g in range(SSD_GROUPS):
        bg = bm[:, g * SSD_STATE:(g + 1) * SSD_STATE]
        cg = cm[:, g * SSD_STATE:(g + 1) * SSD_STATE]
        cb_g = _dot_nt(cg, bg)
        s_old = s_scr[g * grows:(g + 1) * grows, :]
        y_off = _dot_nt(cg, s_old.astype(BF16))
        for r in range(hpg):
            h = g * hpg + r
            hs = slice(h * SSD_HEAD_DIM, (h + 1) * SSD_HEAD_DIM)
            col = a_cs[:, h:h + 1]
            seg = jnp.exp(jnp.where(causal, col - a_cs_t[h:h + 1, :], NEG_INF))
            xh = xs[:, hs]
            xdt = (xh * dt[:, h:h + 1]).astype(BF16)
            y_scr[:, hs] = (_dot((cb_g * seg).astype(BF16), xdt)
                            + jnp.exp(col) * y_off[:, r * SSD_HEAD_DIM:(r + 1) * SSD_HEAD_DIM]
                            + dvec_ref[:, hs] * xh)
            lhs = (xs_t[hs, :] * dtd_t[h:h + 1, :]).astype(BF16)
            s_scr[hs, :] = (jnp.exp(a_cs[q - 1:q, h:h + 1]) * s_old[r * SSD_HEAD_DIM:(r + 1) * SSD_HEAD_DIM, :]
                            + _dot(lhs, bg))

    y_ref[...] = _rms_norm(y_scr[...] * _silu(z_ref[...]), ng_ref[...]).astype(y_ref.dtype)
    xe_scr[0:halo, :] = xbc[q - halo:q, :]

    @pl.when(pl.program_id(1) == pl.num_programs(1) - 1)
    def _():
        sl_ref[0] = s_scr[...]


def _pad_lanes(p2d):
    return jnp.pad(p2d, ((0, 0), (0, LANES - p2d.shape[1])))


def _ssd_head_params(w, layer):
    d_ssd = w["ssd_norm_g"].shape[0]
    heads = d_ssd // SSD_HEAD_DIM
    assert heads <= LANES
    w_in = w["w_in"]
    wdt = _pad_lanes(w_in[layer, :, w_in.shape[2] - heads:])
    dtb = _pad_lanes(_vec(w["ssd_dt_bias"]))
    alog = _pad_lanes(_vec(w["ssd_A_log"]))
    dvec = _vec(jnp.repeat(w["ssd_D"], SSD_HEAD_DIM))
    return wdt, dtb, alog, dvec


def _ssd_prompt(x2d, proj, bsz, seq, w, layer):
    d_model = x2d.shape[1]
    d_ssd = w["ssd_norm_g"].shape[0]
    d_xbc = w["ssd_conv_b"].shape[0]
    q = SSD_CHUNK
    nc = seq // q
    assert seq % q == 0 and (2 * d_ssd + d_ssd) % d_xbc == 0 and d_model == d_ssd
    z_blk = 2 * d_model // d_ssd
    xbc_blk = (2 * d_model + d_ssd) // d_xbc
    wdt, dtb, alog, dvec = _ssd_head_params(w, layer)
    row = lambda b, c: b * nc + c
    lane_spec = pl.BlockSpec((1, LANES), lambda b, c: (0, 0))
    return pl.pallas_call(
        _ssd_prompt_kernel,
        grid=(bsz, nc),
        in_specs=[pl.BlockSpec((q, d_model), lambda b, c: (row(b, c), 0)),
                  pl.BlockSpec((q, d_ssd), lambda b, c: (row(b, c), z_blk)),
                  pl.BlockSpec((q, d_xbc), lambda b, c: (row(b, c), xbc_blk)),
                  pl.BlockSpec((d_model, LANES), lambda b, c: (0, 0)), lane_spec, lane_spec,
                  pl.BlockSpec((CONV_W, d_xbc), lambda b, c: (0, 0)),
                  pl.BlockSpec((1, d_xbc), lambda b, c: (0, 0)),
                  pl.BlockSpec((1, d_ssd), lambda b, c: (0, 0)),
                  pl.BlockSpec((1, d_ssd), lambda b, c: (0, 0))],
        out_specs=[pl.BlockSpec((q, d_ssd), lambda b, c: (row(b, c), 0)),
                   pl.BlockSpec((1, d_ssd, SSD_STATE), lambda b, c: (b, 0, 0))],
        out_shape=[jax.ShapeDtypeStruct((bsz * seq, d_ssd), BF16),
                   jax.ShapeDtypeStruct((bsz, d_ssd, SSD_STATE), F32)],
        scratch_shapes=[pltpu.VMEM((SUBLANES + q, d_xbc), F32), pltpu.VMEM((d_ssd, SSD_STATE), F32),
                        pltpu.VMEM((q, d_ssd), F32)],
        compiler_params=_params("parallel", "arbitrary"),
        name="ssd_prompt",
    )(x2d, proj, proj, wdt, dtb, alog, w["ssd_conv_w"], _vec(w["ssd_conv_b"]), dvec, _vec(w["ssd_norm_g"]))


def _ssd_sample_prep_kernel(x_ref, xbc_ref, buf_ref, wdt_ref, dtb_ref, alog_ref, cw_ref, cb_ref,
                            xs_ref, xdt_t_ref, b_ref, c_t_ref, dec_ref, xdt_scr):
    n, d_xbc = xbc_ref.shape
    d_ssd = xs_ref.shape[1]
    gw = SSD_GROUPS * SSD_STATE
    conv = cb_ref[...]
    for k in range(CONV_W - 1):
        conv = conv + cw_ref[k:k + 1, :] * buf_ref[:, k * d_xbc:(k + 1) * d_xbc]
    act = _silu(conv + cw_ref[CONV_W - 1:CONV_W, :] * xbc_ref[...])
    xs = act[:, :d_ssd]
    xs_ref[...] = xs
    b_ref[...] = act[:, d_ssd:d_ssd + gw]
    c_t_ref[...] = act[:, d_ssd + gw:].T
    dt = _softplus(_dot(x_ref[...].astype(BF16), wdt_ref[...].astype(BF16)) + dtb_ref[...])
    dec_ref[...] = jnp.exp(dt * (-jnp.exp(alog_ref[...])))
    for h in range(d_ssd // SSD_HEAD_DIM):
        hs = slice(h * SSD_HEAD_DIM, (h + 1) * SSD_HEAD_DIM)
        xdt_scr[:, hs] = xs[:, hs] * dt[:, h:h + 1]
    xdt_t_ref[...] = xdt_scr[...].T.astype(BF16)


def _ssd_sample_state_kernel(dec_ref, h0_ref, xdt_t_ref, b_ref, c_t_ref, xs_ref, z_ref, dvec_ref, ng_ref,
                             hn_ref, y_ref, yt_scr):
    bs, d_ssd, _ = h0_ref.shape
    n = xs_ref.shape[0]
    hpg = d_ssd // SSD_HEAD_DIM // SSD_GROUPS
    grows = hpg * SSD_HEAD_DIM
    step = pl.program_id(0)

    @pl.when(step == 0)
    def _():
        yt_scr[...] = jnp.zeros_like(yt_scr)

    seq_rows = lax.broadcasted_iota(jnp.int32, (n, SSD_STATE), 0)
    seq_lanes = lax.broadcasted_iota(jnp.int32, (SSD_STATE, n), 1)
    for j in range(bs):
        b = step * bs + j
        for g in range(SSD_GROUPS):
            gs = slice(g * SSD_STATE, (g + 1) * SSD_STATE)
            b_sel = jnp.where(seq_rows == b, b_ref[:, gs], 0.0).astype(BF16)
            c_sel = jnp.where(seq_lanes == b, c_t_ref[gs, :], 0.0).astype(BF16)
            outer = _dot(xdt_t_ref[g * grows:(g + 1) * grows, :], b_sel)
            for r in range(hpg):
                h = g * hpg + r
                hs = slice(h * SSD_HEAD_DIM, (h + 1) * SSD_HEAD_DIM)
                hn = dec_ref[b, h] * h0_ref[j, hs, :] + outer[r * SSD_HEAD_DIM:(r + 1) * SSD_HEAD_DIM, :]
                hn_ref[j, hs, :] = hn
                yt_scr[hs, :] += _dot(hn.astype(BF16), c_sel)

    @pl.when(step == pl.num_programs(0) - 1)
    def _():
        y = yt_scr[...].T + dvec_ref[...] * xs_ref[...]
        y_ref[...] = _rms_norm(y * _silu(z_ref[...]), ng_ref[...]).astype(y_ref.dtype)


def _ssd_sample(x2d, proj, buf, h0, w, layer, *, bs):
    n, d_model = x2d.shape
    d_ssd = w["ssd_norm_g"].shape[0]
    d_xbc = w["ssd_conv_b"].shape[0]
    gw = SSD_GROUPS * SSD_STATE
    assert n == SSD_STATE == LANES and n % bs == 0
    z_blk = 2 * d_model // d_ssd
    xbc_blk = (2 * d_model + d_ssd) // d_xbc
    wdt, dtb, alog, dvec = _ssd_head_params(w, layer)
    lane_spec = pl.BlockSpec((1, LANES), lambda i: (0, 0))
    xs, xdt_t, bmat, c_t, dec = pl.pallas_call(
        _ssd_sample_prep_kernel,
        grid=(1,),
        in_specs=[pl.BlockSpec((n, d_model), lambda i: (0, 0)),
                  pl.BlockSpec((n, d_xbc), lambda i: (0, xbc_blk)),
                  pl.BlockSpec((n, (CONV_W - 1) * d_xbc), lambda i: (0, 0)),
                  pl.BlockSpec((d_model, LANES), lambda i: (0, 0)), lane_spec, lane_spec,
                  pl.BlockSpec((CONV_W, d_xbc), lambda i: (0, 0)),
                  pl.BlockSpec((1, d_xbc), lambda i: (0, 0))],
        out_specs=[pl.BlockSpec((n, d_ssd), lambda i: (0, 0)),
                   pl.BlockSpec((d_ssd, n), lambda i: (0, 0)),
                   pl.BlockSpec((n, gw), lambda i: (0, 0)),
                   pl.BlockSpec((gw, n), lambda i: (0, 0)),
                   pl.BlockSpec((n, LANES), lambda i: (0, 0))],
        out_shape=[jax.ShapeDtypeStruct((n, d_ssd), F32), jax.ShapeDtypeStruct((d_ssd, n), BF16),
                   jax.ShapeDtypeStruct((n, gw), F32), jax.ShapeDtypeStruct((gw, n), F32),
                   jax.ShapeDtypeStruct((n, LANES), F32)],
        scratch_shapes=[pltpu.VMEM((n, d_ssd), F32)],
        compiler_params=_params("arbitrary"),
        name="ssd_sample_prep",
    )(x2d, proj, buf.reshape(n, (CONV_W - 1) * d_xbc), wdt, dtb, alog, w["ssd_conv_w"], _vec(w["ssd_conv_b"]))

    const = lambda shape: pl.BlockSpec(shape, lambda i: (0,) * len(shape))
    hn, y = pl.pallas_call(
        _ssd_sample_state_kernel,
        grid=(n // bs,),
        in_specs=[pl.BlockSpec(memory_space=pltpu.SMEM),
                  pl.BlockSpec((bs, d_ssd, SSD_STATE), lambda i: (i, 0, 0)),
                  const((d_ssd, n)), const((n, gw)), const((gw, n)), const((n, d_ssd)),
                  pl.BlockSpec((n, d_ssd), lambda i: (0, z_blk)), const((1, d_ssd)), const((1, d_ssd))],
        out_specs=[pl.BlockSpec((bs, d_ssd, SSD_STATE), lambda i: (i, 0, 0)), const((n, d_ssd))],
        out_shape=[jax.ShapeDtypeStruct((n, d_ssd, SSD_STATE), F32), jax.ShapeDtypeStruct((n, d_ssd), BF16)],
        scratch_shapes=[pltpu.VMEM((d_ssd, n), F32)],
        compiler_params=_params("arbitrary"),
        name="ssd_sample_state",
    )(dec, h0.reshape(n, d_ssd, SSD_STATE), xdt_t, bmat, c_t, xs, proj, dvec, _vec(w["ssd_norm_g"]))
    return y, hn


def _attn_prompt_kernel(q_ref, k_ref, v_ref, o_ref):
    d = q_ref.shape[1]
    hd = d // MEM_HEADS
    scale = 1.0 / math.sqrt(hd)
    for h in range(MEM_HEADS):
        sl = slice(h * hd, (h + 1) * hd)
        s = _dot_nt(q_ref[:, sl], k_ref[:, sl].astype(BF16)) * scale
        e = jnp.exp(s - jnp.max(s, -1, keepdims=True))
        p = e / jnp.sum(e, -1, keepdims=True)
        o_ref[:, sl] = _dot(p.astype(BF16), v_ref[:, sl].astype(BF16)).astype(o_ref.dtype)


def _attn_prompt(q, mk, mv, bsz, seq, mem_len, *, tq):
    d = q.shape[1]
    nt = seq // tq
    assert seq % tq == 0
    kv_spec = pl.BlockSpec((mem_len, d), lambda b, t: (b, 0))
    return pl.pallas_call(
        _attn_prompt_kernel,
        grid=(bsz, nt),
        in_specs=[pl.BlockSpec((tq, d), lambda b, t: (b * nt + t, 0)), kv_spec, kv_spec],
        out_specs=pl.BlockSpec((tq, d), lambda b, t: (b * nt + t, 0)),
        out_shape=jax.ShapeDtypeStruct((bsz * seq, d), BF16),
        compiler_params=_params("parallel", "arbitrary"),
        name="attn_prompt",
    )(q, mk, mv)


def _attn_sample_kernel(q_ref, k_ref, v_ref, o_ref):
    bs, _, d = q_ref.shape
    hd = d // MEM_HEADS
    scale = 1.0 / math.sqrt(hd)
    for j in range(bs):
        prod = k_ref[j] * q_ref[j]
        for h in range(MEM_HEADS):
            sl = slice(h * hd, (h + 1) * hd)
            s = jnp.sum(prod[:, sl], -1, keepdims=True) * scale
            e = jnp.exp(s - jnp.max(s, 0, keepdims=True))
            p = e / jnp.sum(e, 0, keepdims=True)
            o_ref[j, :, sl] = jnp.sum(p * v_ref[j, :, sl], 0, keepdims=True)


def _attn_sample(q, ck, cv, *, bs):
    n, d = q.shape
    mem_len = ck.shape[1]
    assert n % bs == 0
    kv_spec = pl.BlockSpec((bs, mem_len, d), lambda i: (i, 0, 0))
    q_spec = pl.BlockSpec((bs, 1, d), lambda i: (i, 0, 0))
    return pl.pallas_call(
        _attn_sample_kernel,
        grid=(n // bs,),
        in_specs=[q_spec, kv_spec, kv_spec],
        out_specs=q_spec,
        out_shape=jax.ShapeDtypeStruct((n, 1, d), F32),
        compiler_params=_params("parallel"),
        name="attn_sample",
    )(q.reshape(n, 1, d), ck.reshape(n, mem_len, d), cv.reshape(n, mem_len, d)).reshape(n, d)


def _top_values(s):
    vals = []
    work = s
    for _ in range(PEER_TOPK):
        m = jnp.max(work, 0, keepdims=True)
        vals.append(m)
        work = jnp.where(work == m, NEG_INF, work)
    return vals


def _count_ge(s, v):
    return jnp.sum(jnp.where(s >= v, 1.0, 0.0), 0, keepdims=True)


def _multiplicities(s, vals):
    return _stack_rows([jnp.sum(jnp.where(s == v, 1.0, 0.0), 0, keepdims=True) for v in vals])


def _stack_rows(rows):
    ridx = lax.broadcasted_iota(jnp.int32, (len(rows), rows[0].shape[1]), 0)
    out = jnp.broadcast_to(rows[0], ridx.shape)
    for i in range(1, len(rows)):
        out = jnp.where(ridx == i, rows[i], out)
    return out


def _peer_route_kernel(q_ref, sk_ref, c1_ref, d1_ref, s2_ref, e2_ref, s1_scr, top_scr, cnt_scr):
    half = sk_ref.shape[3]
    n = q_ref.shape[0]
    row8 = lax.broadcasted_iota(jnp.int32, (SUBLANES, n), 0)
    lo, hi = slice(0, SUBLANES), slice(SUBLANES, PEER_TOPK)
    rows = lambda stacked: [stacked[k:k + 1] for k in range(PEER_TOPK)]

    most = None
    for h in range(PEER_HEADS):
        s1, s2 = [_dot_nt(sk_ref[h, c].astype(BF16), q_ref[:, (2 * h + c) * half:(2 * h + c + 1) * half])
                  for c in range(2)]
        va, vb = _top_values(s1), _top_values(s2)
        s1_scr[h] = s1
        s2_ref[h] = s2
        top_scr[h, 0] = _stack_rows(va)
        top_scr[h, 1] = _stack_rows(vb)
        ge = jnp.maximum(_count_ge(s1, va[-1]), _count_ge(s2, vb[-1]))
        most = ge if most is None else jnp.maximum(most, ge)

    cnt_scr[...] = jnp.ones_like(cnt_scr)

    @pl.when(jnp.max(most) > PEER_TOPK)
    def _():
        for h in range(PEER_HEADS):
            cnt_scr[h, 0] = _multiplicities(s1_scr[h], rows(top_scr[h, 0]))
            cnt_scr[h, 1] = _multiplicities(s2_ref[h], rows(top_scr[h, 1]))

    for h in range(PEER_HEADS):
        s1, s2 = s1_scr[h], s2_ref[h]
        ca_all, cb_all = cnt_scr[h, 0], cnt_scr[h, 1]
        va_all, vb_all = top_scr[h, 0], top_scr[h, 1]
        va, vb = rows(va_all), rows(vb_all)
        blocks = [va[0] + vb_all[lo], va[0] + vb_all[hi]]
        mults = [ca_all[0:1] * cb_all[lo], ca_all[0:1] * cb_all[hi]]
        for k1 in range(1, SUBLANES):
            blocks.append(jnp.where(row8 < PEER_TOPK // (k1 + 1), va[k1] + vb_all[lo], NEG_INF))
            mults.append(ca_all[k1:k1 + 1] * cb_all[lo])
        blocks.append(va_all[hi] + vb[0])
        mults.append(ca_all[hi] * cb_all[0:1])
        top = va[0] + vb[0]
        tau = top
        seen = jnp.zeros_like(top)
        work = blocks
        for _ in range(PEER_TOPK):
            m = jnp.max(functools.reduce(jnp.maximum, work), 0, keepdims=True)
            eq = [blk == m for blk in work]
            tau = jnp.where(seen < PEER_TOPK, m, tau)
            hit = functools.reduce(jnp.add, [jnp.where(q, c, 0.0) for q, c in zip(eq, mults)])
            seen = seen + jnp.sum(hit, 0, keepdims=True)
            work = [jnp.where(q, NEG_INF, blk) for q, blk in zip(eq, work)]
        zs = functools.reduce(jnp.add, [jnp.where(blk >= tau, c * jnp.exp(blk - top), 0.0)
                                        for blk, c in zip(blocks, mults)])
        z = jnp.sum(zs, 0, keepdims=True)
        cut = [jnp.min(jnp.minimum(jnp.where(blocks[0] >= tau, vb_all[lo], jnp.inf),
                                   jnp.where(blocks[1] >= tau, vb_all[hi], jnp.inf)), 0, keepdims=True)]
        for k1 in range(1, SUBLANES):
            cut.append(jnp.min(jnp.where(blocks[k1 + 1] >= tau, vb_all[lo], jnp.inf), 0, keepdims=True))
        cut_tail = jnp.where(blocks[-1] >= tau, vb[0], jnp.inf)
        d1 = jnp.full_like(s1, jnp.inf)
        for k1 in range(PEER_TOPK):
            c = cut[k1] if k1 < SUBLANES else cut_tail[k1 - SUBLANES:k1 - SUBLANES + 1]
            d1 = jnp.where(s1 == va[k1], c, d1)
        c1_ref[h] = jnp.exp(s1 - va[0]) * (1.0 / z)
        d1_ref[h] = d1
        s2_ref[h] = s2
        e2_ref[h] = jnp.exp(s2 - vb[0])


def _peer_route(q, subkeys, layer, *, tn):
    n = q.shape[0]
    _, heads, _, nkeys, half = subkeys.shape
    assert heads == PEER_HEADS and nkeys == PEER_NKEYS and n % tn == 0 and PEER_TOPK == 2 * SUBLANES
    score_spec = pl.BlockSpec((heads, nkeys, tn), lambda i: (0, 0, i))
    score_shape = jax.ShapeDtypeStruct((heads, nkeys, n), F32)
    return pl.pallas_call(
        _peer_route_kernel,
        grid=(n // tn,),
        in_specs=[pl.BlockSpec((tn, q.shape[1]), lambda i: (i, 0)),
                  pl.BlockSpec((None,) + subkeys.shape[1:], lambda i: (layer, 0, 0, 0, 0))],
        out_specs=[score_spec] * 4,
        out_shape=[score_shape] * 4,
        scratch_shapes=[pltpu.VMEM((heads, nkeys, tn), F32), pltpu.VMEM((heads, 2, PEER_TOPK, tn), F32),
                        pltpu.VMEM((heads, 2, PEER_TOPK, tn), F32)],
        compiler_params=_params("parallel"),
        name="peer_route",
    )(q, subkeys)


def _peer_gate_tile(c1_ref, d1_ref, s2_ref, e2_ref, w_ref, e_tile):
    tn, et = w_ref.shape
    for ii in range(et // PEER_NKEYS):
        i1 = e_tile * (et // PEER_NKEYS) + ii
        cut = [d1_ref[h, pl.ds(i1, 1), :] for h in range(PEER_HEADS)]
        fac = [c1_ref[h, pl.ds(i1, 1), :] for h in range(PEER_HEADS)]
        for lt in range(tn // LANES):
            ts = slice(lt * LANES, (lt + 1) * LANES)
            w_t = None
            for h in range(PEER_HEADS):
                hit = s2_ref[h, :, ts] >= cut[h][:, ts]
                term = jnp.where(hit, e2_ref[h, :, ts], 0.0) * fac[h][:, ts]
                w_t = term if w_t is None else w_t + term
            w_ref[ts, ii * PEER_NKEYS:(ii + 1) * PEER_NKEYS] = w_t.T


def _peer_main_kernel(x_ref, c1_ref, d1_ref, s2_ref, e2_ref, u_ref, v_ref, g_ref, b_ref,
                      o_ref, xbf_scr, w_scr, *, alpha):
    e = pl.program_id(1)
    last = pl.num_programs(1) - 1
    route = (c1_ref, d1_ref, s2_ref, e2_ref)

    @pl.when(e == 0)
    def _():
        xbf_scr[...] = x_ref[...].astype(BF16)
        o_ref[...] = jnp.zeros_like(o_ref)
        _peer_gate_tile(*route, w_scr.at[0], 0)

    _peer_gate_tile(*route, w_scr.at[(e + 1) % 2], jnp.minimum(e + 1, last))
    act = _dot_nt(xbf_scr[...], u_ref[...])
    m = (_gelu(act) * w_scr[e % 2]).astype(BF16)
    o_ref[...] += _dot(m, v_ref[...])

    @pl.when(e == last)
    def _():
        o_ref[...] = _layer_norm(alpha * x_ref[...] + o_ref[...], g_ref[...], b_ref[...])


def _cast_kernel(x_ref, o_ref):
    o_ref[...] = x_ref[...].astype(o_ref.dtype)


def _to_bf16(w, layer, *, rows):
    _, r, c = w.shape
    assert r % rows == 0
    return pl.pallas_call(
        _cast_kernel,
        grid=(r // rows,),
        in_specs=[pl.BlockSpec((None, rows, c), lambda i: (layer, i, 0))],
        out_specs=pl.BlockSpec((rows, c), lambda i: (i, 0)),
        out_shape=jax.ShapeDtypeStruct((r, c), BF16),
        compiler_params=_params("parallel"),
        name="to_bf16",
    )(w)


def _peer_main(x, route, u, v, g, b, *, alpha, tn, et):
    n, d = x.shape
    n_exp = u.shape[0]
    assert n_exp == PEER_NKEYS * PEER_NKEYS and n_exp % et == 0 and et % PEER_NKEYS == 0
    assert n % tn == 0 and tn % LANES == 0 and u.dtype == BF16 and v.dtype == BF16
    once = pl.Buffered(1)
    score_spec = pl.BlockSpec((PEER_HEADS, PEER_NKEYS, tn), lambda i, e: (0, 0, i), pipeline_mode=once)
    expert_spec = pl.BlockSpec((et, d), lambda i, e: (e, 0))
    return pl.pallas_call(
        functools.partial(_peer_main_kernel, alpha=alpha),
        grid=(n // tn, n_exp // et),
        in_specs=[pl.BlockSpec((tn, d), lambda i, e: (i, 0), pipeline_mode=once),
                  score_spec, score_spec, score_spec, score_spec, expert_spec, expert_spec,
                  pl.BlockSpec((1, d), lambda i, e: (0, 0)),
                  pl.BlockSpec((1, d), lambda i, e: (0, 0))],
        out_specs=pl.BlockSpec((tn, d), lambda i, e: (i, 0), pipeline_mode=once),
        out_shape=jax.ShapeDtypeStruct((n, d), F32),
        scratch_shapes=[pltpu.VMEM((tn, d), BF16), pltpu.VMEM((2, tn, et), F32)],
        compiler_params=_params("parallel", "arbitrary"),
        name="peer_main",
    )(x, *route, u, v, _vec(g), _vec(b))


def _tiles(n_tokens):
    return dict(mm_rows=min(n_tokens, 2048), mm_cols=512, ln_rows=min(n_tokens, 1024), ln_k=512,
                peer_rows=min(n_tokens, 512), peer_experts=512, route_rows=LANES,
                lru_rows=256, attn_rows=512, ssd_state_seqs=4, attn_seqs=2)


def _token_stages(x1_in, attn_fn, w, layer, alpha, t):
    q = _mm(x1_in, w["mem_wq"], layer, out_dtype=attn_fn.q_dtype, tm=t["mm_rows"], tn=t["mm_cols"],
            name="mem_q")
    o = attn_fn(q)
    x2 = _mm_ln([o], w["mem_wo"], layer, x1_in, w["ln2_g"], w["ln2_b"], alpha=alpha, tm=t["ln_rows"],
                tk=t["ln_k"], name="mem_out_ln")
    pq = _mm(x2, w["peer_wq"], layer, out_dtype=BF16, tm=t["mm_rows"], tn=t["mm_cols"], name="peer_q")
    route = _peer_route(pq, w["peer_subkeys"], layer, tn=t["route_rows"])
    return _peer_main(x2, route, w["peer_u_bf16"], w["peer_v_bf16"], w["ln3_g"], w["ln3_b"], alpha=alpha,
                      tn=t["peer_rows"], et=t["peer_experts"])


class _AttnFn:
    def __init__(self, fn, q_dtype):
        self.fn, self.q_dtype = fn, q_dtype

    def __call__(self, q):
        return self.fn(q)


def _prompt_layer(x, mem, w, layer, alpha):
    bsz, seq, d = x.shape
    x2d = x.reshape(bsz * seq, d)
    d_lru = w["lru_lambda"].shape[0]
    d_ssd = w["ssd_norm_g"].shape[0]
    d_xbc = w["ssd_conv_b"].shape[0]
    n_proj = 2 * d_lru + d_ssd + d_xbc
    assert seq >= CONV_W - 1
    t = _tiles(bsz * seq)
    proj = _mm(x2d, w["w_in"], layer, ncols=n_proj, tm=t["mm_rows"], tn=t["mm_cols"], name="in_proj")
    y_lru, lru_h = _lru_prompt(proj, bsz, seq, w, layer, tt=t["lru_rows"])
    y_ssd, ssd_h = _ssd_prompt(x2d, proj, bsz, seq, w, layer)
    tail = proj.reshape(bsz, seq, n_proj)[:, seq - (CONV_W - 1):]
    lru_conv = tail[:, :, :d_lru]
    ssd_conv = tail[:, :, n_proj - d_xbc:]
    x1 = _mm_ln([y_lru, y_ssd], w["w_out"], layer, x2d, w["ln1_g"], w["ln1_b"], alpha=alpha,
                tm=t["ln_rows"], tk=t["ln_k"], name="mix_out_ln")
    mem_len = mem.shape[1]
    mem2d = mem.reshape(bsz * mem_len, d)
    tmem = _tiles(bsz * mem_len)
    mk = _mm(mem2d, w["mem_wk"], layer, tm=tmem["mm_rows"], tn=tmem["mm_cols"], name="mem_k")
    mv = _mm(mem2d, w["mem_wv"], layer, tm=tmem["mm_rows"], tn=tmem["mm_cols"], name="mem_v")
    attn = _AttnFn(lambda q: _attn_prompt(q, mk, mv, bsz, seq, mem_len, tq=t["attn_rows"]), BF16)
    y = _token_stages(x1, attn, w, layer, alpha, t)
    hd = d // MEM_HEADS
    states = (lru_conv, lru_h.reshape(bsz, d_lru), ssd_conv,
              ssd_h.reshape(bsz, d_ssd // SSD_HEAD_DIM, SSD_HEAD_DIM, SSD_STATE),
              mk.reshape(bsz, mem_len, MEM_HEADS, hd), mv.reshape(bsz, mem_len, MEM_HEADS, hd))
    return y.reshape(bsz, seq, d), states


def _sample_layer(x, ck, cv, lru_buf, lru_h, ssd_buf, ssd_h, w, layer, alpha):
    n, seq, d = x.shape
    assert seq == 1
    x2d = x.reshape(n, d)
    d_lru = w["lru_lambda"].shape[0]
    d_ssd = w["ssd_norm_g"].shape[0]
    d_xbc = w["ssd_conv_b"].shape[0]
    n_proj = 2 * d_lru + d_ssd + d_xbc
    t = _tiles(n)
    proj = _mm(x2d, w["w_in"], layer, ncols=n_proj, tm=t["mm_rows"], tn=t["mm_cols"], name="in_proj_s")
    y_lru, lru_hn = _lru_sample(proj, lru_buf, lru_h, w, layer)
    y_ssd, ssd_hn = _ssd_sample(x2d, proj, ssd_buf, ssd_h, w, layer, bs=t["ssd_state_seqs"])
    lru_conv = jnp.concatenate([lru_buf[:, 1:], proj[:, None, :d_lru]], 1)
    ssd_conv = jnp.concatenate([ssd_buf[:, 1:], proj[:, None, n_proj - d_xbc:]], 1)
    x1 = _mm_ln([y_lru, y_ssd], w["w_out"], layer, x2d, w["ln1_g"], w["ln1_b"], alpha=alpha,
                tm=t["ln_rows"], tk=t["ln_k"], name="mix_out_ln_s")
    attn = _AttnFn(lambda q: _attn_sample(q, ck, cv, bs=t["attn_seqs"]), F32)
    y = _token_stages(x1, attn, w, layer, alpha, t)
    states = (lru_conv, lru_hn, ssd_conv, ssd_hn.reshape(ssd_h.shape))
    return y.reshape(n, 1, d), states


def kernel(x_prompt, x_sample, mem_prompt, state_lru_conv, state_lru_h, state_ssd_conv, state_ssd_h,
           cache_mem_k, cache_mem_v, w_in, lru_conv_w, lru_conv_b, lru_gate_a_w, lru_gate_a_b,
           lru_gate_x_w, lru_gate_x_b, lru_lambda, lru_norm_g, ssd_conv_w, ssd_conv_b, ssd_dt_bias,
           ssd_A_log, ssd_D, ssd_norm_g, w_out, ln1_g, ln1_b, mem_wq, mem_wk, mem_wv, mem_wo,
           ln2_g, ln2_b, peer_wq, peer_subkeys, peer_u, peer_v, ln3_g, ln3_b):
    depth = w_in.shape[0]
    alpha = (2.0 * depth) ** 0.25
    weights = dict(w_in=w_in, lru_conv_w=lru_conv_w, lru_conv_b=lru_conv_b, lru_gate_a_w=lru_gate_a_w,
                   lru_gate_a_b=lru_gate_a_b, lru_gate_x_w=lru_gate_x_w, lru_gate_x_b=lru_gate_x_b,
                   lru_lambda=lru_lambda, lru_norm_g=lru_norm_g, ssd_conv_w=ssd_conv_w,
                   ssd_conv_b=ssd_conv_b, ssd_dt_bias=ssd_dt_bias, ssd_A_log=ssd_A_log, ssd_D=ssd_D,
                   ssd_norm_g=ssd_norm_g, w_out=w_out, ln1_g=ln1_g, ln1_b=ln1_b, mem_wq=mem_wq,
                   mem_wk=mem_wk, mem_wv=mem_wv, mem_wo=mem_wo, ln2_g=ln2_g, ln2_b=ln2_b,
                   peer_wq=peer_wq, peer_subkeys=peer_subkeys, peer_u=peer_u, peer_v=peer_v,
                   ln3_g=ln3_g, ln3_b=ln3_b)
    yp, ys = x_prompt, x_sample
    p_states, s_states = [], []
    stacked = ("w_in", "w_out", "mem_wq", "mem_wk", "mem_wv", "mem_wo", "peer_wq", "peer_subkeys",
               "peer_u", "peer_v", "lru_gate_a_w", "lru_gate_x_w")
    for l in range(depth):
        w = {name: (p if name in stacked else p[l]) for name, p in weights.items()}
        w["peer_u_bf16"] = _to_bf16(peer_u, l, rows=CAST_ROWS)
        w["peer_v_bf16"] = _to_bf16(peer_v, l, rows=CAST_ROWS)
        yp, st = _prompt_layer(yp, mem_prompt, w, l, alpha)
        p_states.append(st)
        ys, st = _sample_layer(ys, cache_mem_k[l], cache_mem_v[l], state_lru_conv[l], state_lru_h[l],
                               state_ssd_conv[l], state_ssd_h[l], w, l, alpha)
        s_states.append(st)
    stack = lambda states, i: jnp.stack([st[i] for st in states])
    return (yp, ys) + tuple(stack(p_states, i) for i in range(6)) + tuple(stack(s_states, i) for i in range(4))
```

```python
import functools
import math

import jax
import jax.numpy as jnp
from jax import lax
from jax.experimental import pallas as pl
from jax.experimental.pallas import tpu as pltpu

F32 = jnp.float32
BF16 = jnp.bfloat16

LANES = 128
SUBLANES = 8
VMEM_LIMIT_BYTES = 56 * 1024 * 1024
CAST_ROWS = 1024

CONV_W = 4
LRU_BLOCK = 128
LRU_C = 8.0
SSD_HEAD_DIM = 64
SSD_GROUPS = 4
SSD_STATE = 128
SSD_CHUNK = 128
MEM_HEADS = 4
PEER_HEADS = 8
PEER_NKEYS = 128
PEER_TOPK = 16
LN_EPS = 1e-5
RMS_EPS = 1e-6
NEG_INF = float("-inf")


def _params(*sem):
    return pltpu.CompilerParams(dimension_semantics=sem, vmem_limit_bytes=VMEM_LIMIT_BYTES)


def _dot(a, b):
    return jnp.dot(a, b, preferred_element_type=F32)


def _dot_nt(a, b):
    return lax.dot_general(a, b, (((1,), (1,)), ((), ())), preferred_element_type=F32)


def _dot_split3(m01, x):
    hi = x.astype(BF16)
    r1 = x - hi.astype(F32)
    mid = r1.astype(BF16)
    lo = (r1 - mid.astype(F32)).astype(BF16)
    return _dot(m01, hi) + _dot(m01, mid) + _dot(m01, lo)


def _sigmoid(x):
    return jax.nn.sigmoid(x)


def _silu(x):
    return x * _sigmoid(x)


def _gelu(x):
    return 0.5 * x * (1.0 + lax.erf(x * (1.0 / math.sqrt(2.0))))


def _softplus(x):
    return jnp.maximum(x, 0.0) + jnp.log1p(jnp.exp(-jnp.abs(x)))


def _layer_norm(y, g, b):
    mu = jnp.mean(y, -1, keepdims=True)
    d = y - mu
    var = jnp.mean(d * d, -1, keepdims=True)
    return d * lax.rsqrt(var + LN_EPS) * g + b


def _rms_norm(v, g):
    return v * lax.rsqrt(jnp.mean(v * v, -1, keepdims=True) + RMS_EPS) * g


def _mm_kernel(a_ref, w_ref, o_ref, abf_ref):
    @pl.when(pl.program_id(1) == 0)
    def _():
        abf_ref[...] = a_ref[...].astype(BF16)

    o_ref[...] = _dot(abf_ref[...], w_ref[...].astype(BF16)).astype(o_ref.dtype)


def _mm(a, w, layer, *, ncols=None, out_dtype=F32, tm, tn, name):
    m, k = a.shape
    n = ncols or w.shape[2]
    assert m % tm == 0 and n % tn == 0 and w.shape[1] == k
    return pl.pallas_call(
        _mm_kernel,
        grid=(m // tm, n // tn),
        in_specs=[pl.BlockSpec((tm, k), lambda i, j: (i, 0), pipeline_mode=pl.Buffered(1)),
                  pl.BlockSpec((None, k, tn), lambda i, j: (layer, 0, j))],
        out_specs=pl.BlockSpec((tm, tn), lambda i, j: (i, j)),
        out_shape=jax.ShapeDtypeStruct((m, n), out_dtype),
        scratch_shapes=[pltpu.VMEM((tm, k), BF16)],
        compiler_params=_params("parallel", "arbitrary"),
        name=name,
    )(a, w)


def _mm_ln_kernel(*refs, n_a, nk_each, alpha):
    a_refs = refs[:n_a]
    w_ref, res_ref, g_ref, b_ref, o_ref = refs[n_a:]
    k = pl.program_id(1)

    @pl.when(k == 0)
    def _():
        o_ref[...] = jnp.zeros_like(o_ref)

    for idx in range(n_a):
        @pl.when((k >= idx * nk_each) & (k < (idx + 1) * nk_each))
        def _(idx=idx):
            o_ref[...] += _dot(a_refs[idx][...].astype(BF16), w_ref[...].astype(BF16))

    @pl.when(k == n_a * nk_each - 1)
    def _():
        o_ref[...] = _layer_norm(alpha * res_ref[...] + o_ref[...], g_ref[...], b_ref[...])


def _mm_ln(a_list, w, layer, res, g, b, *, alpha, tm, tk, name):
    m, n = res.shape
    ka = a_list[0].shape[1]
    n_a = len(a_list)
    assert all(a.shape == (m, ka) for a in a_list) and w.shape[1:] == (n_a * ka, n)
    assert m % tm == 0 and ka % tk == 0
    nk_each = ka // tk

    def a_map(idx):
        return lambda i, k: (i, jnp.clip(k - idx * nk_each, 0, nk_each - 1))

    return pl.pallas_call(
        functools.partial(_mm_ln_kernel, n_a=n_a, nk_each=nk_each, alpha=alpha),
        grid=(m // tm, n_a * nk_each),
        in_specs=[pl.BlockSpec((tm, tk), a_map(idx)) for idx in range(n_a)] + [
            pl.BlockSpec((None, tk, n), lambda i, k: (layer, k, 0)),
            pl.BlockSpec((tm, n), lambda i, k: (i, 0)),
            pl.BlockSpec((1, n), lambda i, k: (0, 0)),
            pl.BlockSpec((1, n), lambda i, k: (0, 0))],
        out_specs=pl.BlockSpec((tm, n), lambda i, k: (i, 0)),
        out_shape=jax.ShapeDtypeStruct((m, n), F32),
        compiler_params=_params("parallel", "arbitrary"),
        name=name,
    )(*a_list, w, res, g.reshape(1, n), b.reshape(1, n))


def _lru_gate_block(xck, wa_ref, wx_ref, ba, bx, sp, kblk):
    xb = xck.astype(BF16)
    r = _sigmoid(_dot(xb, wa_ref[kblk].astype(BF16)) + ba)
    i = _sigmoid(_dot(xb, wx_ref[kblk].astype(BF16)) + bx)
    log_a = -LRU_C * r * sp
    a = jnp.exp(log_a)
    mult = jnp.sqrt(jnp.maximum(1.0 - a * a, 0.0))
    return a, mult * (i * xck)


def _lru_prompt_kernel(x_ref, gt_ref, cw_ref, cb_ref, wa_ref, ba_ref, wx_ref, bx_ref, lam_ref, ng_ref,
                       y_ref, hl_ref, xe_scr, a_scr, u_scr, h_scr):
    tt, c = x_ref.shape
    halo = SUBLANES

    @pl.when(pl.program_id(1) == 0)
    def _():
        xe_scr[0:halo, :] = jnp.zeros((halo, c), F32)
        h_scr[...] = jnp.zeros_like(h_scr)

    x = x_ref[...]
    xe_scr[halo:halo + tt, :] = x
    xc = cb_ref[...]
    for k in range(CONV_W - 1):
        off = halo - (CONV_W - 1) + k
        xc = xc + cw_ref[k:k + 1, :] * xe_scr[off:off + tt, :]
    xc = xc + cw_ref[CONV_W - 1:CONV_W, :] * x

    sp = _softplus(-lam_ref[...])
    for kblk in range(c // LRU_BLOCK):
        sl = slice(kblk * LRU_BLOCK, (kblk + 1) * LRU_BLOCK)
        a, u = _lru_gate_block(xc[:, sl], wa_ref, wx_ref, ba_ref[:, sl], bx_ref[:, sl], sp[:, sl], kblk)
        a_scr[:, sl] = a
        u_scr[:, sl] = u

    row = lax.broadcasted_iota(jnp.int32, (SUBLANES, c), 0)

    def scan_rows(i, h):
        r0 = pl.multiple_of(i * SUBLANES, SUBLANES)
        a = a_scr[pl.ds(r0, SUBLANES), :]
        u = u_scr[pl.ds(r0, SUBLANES), :]
        for s in (1, 2, 4):
            keep = row >= s
            a_prev = jnp.where(keep, pltpu.roll(a, s, 0), 1.0)
            u_prev = jnp.where(keep, pltpu.roll(u, s, 0), 0.0)
            u = a * u_prev + u
            a = a * a_prev
        hh = a * h + u
        u_scr[pl.ds(r0, SUBLANES), :] = hh
        return hh[SUBLANES - 1:SUBLANES, :]

    h_last = lax.fori_loop(0, tt // SUBLANES, scan_rows, h_scr[0:1, :])

    y_ref[...] = _rms_norm(u_scr[...] * _gelu(gt_ref[...]), ng_ref[...]).astype(y_ref.dtype)
    xe_scr[0:halo, :] = x[tt - halo:tt, :]
    h_scr[...] = jnp.broadcast_to(h_last, h_scr.shape)
    hl_ref[0] = h_last


def _vec(p):
    return p.reshape(1, -1)


def _lru_prompt(proj, bsz, seq, w, layer, *, tt):
    c = w["lru_lambda"].shape[0]
    nblk = c // LRU_BLOCK
    nt = seq // tt
    assert seq % tt == 0 and tt % SUBLANES == 0 and tt >= SUBLANES
    row = lambda b, t: b * nt + t
    vec_spec = pl.BlockSpec((1, c), lambda b, t: (0, 0))
    gate_w_spec = pl.BlockSpec((None, nblk, LRU_BLOCK, LRU_BLOCK), lambda b, t: (layer, 0, 0, 0))
    return pl.pallas_call(
        _lru_prompt_kernel,
        grid=(bsz, nt),
        in_specs=[pl.BlockSpec((tt, c), lambda b, t: (row(b, t), 0)),
                  pl.BlockSpec((tt, c), lambda b, t: (row(b, t), 1)),
                  pl.BlockSpec((CONV_W, c), lambda b, t: (0, 0)), vec_spec,
                  gate_w_spec, vec_spec, gate_w_spec, vec_spec, vec_spec, vec_spec],
        out_specs=[pl.BlockSpec((tt, c), lambda b, t: (row(b, t), 0)),
                   pl.BlockSpec((1, 1, c), lambda b, t: (b, 0, 0))],
        out_shape=[jax.ShapeDtypeStruct((bsz * seq, c), BF16),
                   jax.ShapeDtypeStruct((bsz, 1, c), F32)],
        scratch_shapes=[pltpu.VMEM((SUBLANES + tt, c), F32), pltpu.VMEM((tt, c), F32),
                        pltpu.VMEM((tt, c), F32), pltpu.VMEM((SUBLANES, c), F32)],
        compiler_params=_params("parallel", "arbitrary"),
        name="lru_prompt",
    )(proj, proj, w["lru_conv_w"], _vec(w["lru_conv_b"]), w["lru_gate_a_w"], _vec(w["lru_gate_a_b"]),
      w["lru_gate_x_w"], _vec(w["lru_gate_x_b"]), _vec(w["lru_lambda"]), _vec(w["lru_norm_g"]))


def _lru_sample_kernel(x_ref, gt_ref, buf_ref, h0_ref, cw_ref, cb_ref, wa_ref, ba_ref, wx_ref, bx_ref,
                       lam_ref, ng_ref, y_ref, hn_ref):
    n, c = x_ref.shape
    xc = cb_ref[...]
    for k in range(CONV_W - 1):
        xc = xc + cw_ref[k:k + 1, :] * buf_ref[:, k * c:(k + 1) * c]
    xc = xc + cw_ref[CONV_W - 1:CONV_W, :] * x_ref[...]
    sp = _softplus(-lam_ref[...])
    for kblk in range(c // LRU_BLOCK):
        sl = slice(kblk * LRU_BLOCK, (kblk + 1) * LRU_BLOCK)
        a, u = _lru_gate_block(xc[:, sl], wa_ref, wx_ref, ba_ref[:, sl], bx_ref[:, sl], sp[:, sl], kblk)
        hn_ref[:, sl] = a * h0_ref[:, sl] + u
    y_ref[...] = _rms_norm(hn_ref[...] * _gelu(gt_ref[...]), ng_ref[...]).astype(y_ref.dtype)


def _lru_sample(proj, buf, h0, w, layer):
    n = proj.shape[0]
    c = w["lru_lambda"].shape[0]
    nblk = c // LRU_BLOCK
    vec_spec = pl.BlockSpec((1, c), lambda i: (0, 0))
    gate_w_spec = pl.BlockSpec((None, nblk, LRU_BLOCK, LRU_BLOCK), lambda i: (layer, 0, 0, 0))
    full = pl.BlockSpec((n, c), lambda i: (0, 0))
    return pl.pallas_call(
        _lru_sample_kernel,
        grid=(1,),
        in_specs=[full, pl.BlockSpec((n, c), lambda i: (0, 1)),
                  pl.BlockSpec((n, (CONV_W - 1) * c), lambda i: (0, 0)), full,
                  pl.BlockSpec((CONV_W, c), lambda i: (0, 0)), vec_spec,
                  gate_w_spec, vec_spec, gate_w_spec, vec_spec, vec_spec, vec_spec],
        out_specs=[full, full],
        out_shape=[jax.ShapeDtypeStruct((n, c), BF16), jax.ShapeDtypeStruct((n, c), F32)],
        compiler_params=_params("arbitrary"),
        name="lru_sample",
    )(proj, proj, buf.reshape(n, (CONV_W - 1) * c), h0, w["lru_conv_w"], _vec(w["lru_conv_b"]),
      w["lru_gate_a_w"], _vec(w["lru_gate_a_b"]), w["lru_gate_x_w"], _vec(w["lru_gate_x_b"]),
      _vec(w["lru_lambda"]), _vec(w["lru_norm_g"]))


def _ssd_prompt_kernel(x_ref, z_ref, xbc_ref, wdt_ref, dtb_ref, alog_ref, cw_ref, cb_ref, dvec_ref, ng_ref,
                       y_ref, sl_ref, xe_scr, s_scr, y_scr):
    q, d_ssd = z_ref.shape
    halo = SUBLANES
    gw = SSD_GROUPS * SSD_STATE
    hpg = d_ssd // SSD_HEAD_DIM // SSD_GROUPS
    grows = hpg * SSD_HEAD_DIM

    @pl.when(pl.program_id(1) == 0)
    def _():
        xe_scr[0:halo, :] = jnp.zeros((halo, xe_scr.shape[1]), F32)
        s_scr[...] = jnp.zeros_like(s_scr)

    xbc = xbc_ref[...]
    xe_scr[halo:halo + q, :] = xbc
    conv = cb_ref[...]
    for k in range(CONV_W - 1):
        off = halo - (CONV_W - 1) + k
        conv = conv + cw_ref[k:k + 1, :] * xe_scr[off:off + q, :]
    act = _silu(conv + cw_ref[CONV_W - 1:CONV_W, :] * xbc)
    xs = act[:, :d_ssd]
    bm = act[:, d_ssd:d_ssd + gw].astype(BF16)
    cm = act[:, d_ssd + gw:].astype(BF16)

    dt = _softplus(_dot(x_ref[...].astype(BF16), wdt_ref[...].astype(BF16)) + dtb_ref[...])
    ag = dt * (-jnp.exp(alog_ref[...]))
    ri = lax.broadcasted_iota(jnp.int32, (q, q), 0)
    ci = lax.broadcasted_iota(jnp.int32, (q, q), 1)
    causal = ri >= ci
    a_cs = _dot_split3(causal.astype(F32).astype(BF16), ag)
    a_cs_t = a_cs.T
    dtd_t = (dt * jnp.exp(a_cs[q - 1:q, :] - a_cs)).T
    xs_t = xs.T

    for g in range(SSD_GROUPS):
        bg = bm[:, g * SSD_STATE:(g + 1) * SSD_STATE]
        cg = cm[:, g * SSD_STATE:(g + 1) * SSD_STATE]
        cb_g = _dot_nt(cg, bg)
        s_old = s_scr[g * grows:(g + 1) * grows, :]
        y_off = _dot_nt(cg, s_old.astype(BF16))
        for r in range(hpg):
            h = g * hpg + r
            hs = slice(h * SSD_HEAD_DIM, (h + 1) * SSD_HEAD_DIM)
            col = a_cs[:, h:h + 1]
            seg = jnp.exp(jnp.where(causal, col - a_cs_t[h:h + 1, :], NEG_INF))
            xh = xs[:, hs]
            xdt = (xh * dt[:, h:h + 1]).astype(BF16)
            y_scr[:, hs] = (_dot((cb_g * seg).astype(BF16), xdt)
                            + jnp.exp(col) * y_off[:, r * SSD_HEAD_DIM:(r + 1) * SSD_HEAD_DIM]
                            + dvec_ref[:, hs] * xh)
            lhs = (xs_t[hs, :] * dtd_t[h:h + 1, :]).astype(BF16)
            s_scr[hs, :] = (jnp.exp(a_cs[q - 1:q, h:h + 1]) * s_old[r * SSD_HEAD_DIM:(r + 1) * SSD_HEAD_DIM, :]
                            + _dot(lhs, bg))

    y_ref[...] = _rms_norm(y_scr[...] * _silu(z_ref[...]), ng_ref[...]).astype(y_ref.dtype)
    xe_scr[0:halo, :] = xbc[q - halo:q, :]

    @pl.when(pl.program_id(1) == pl.num_programs(1) - 1)
    def _():
        sl_ref[0] = s_scr[...]


def _pad_lanes(p2d):
    return jnp.pad(p2d, ((0, 0), (0, LANES - p2d.shape[1])))


def _ssd_head_params(w, layer):
    d_ssd = w["ssd_norm_g"].shape[0]
    heads = d_ssd // SSD_HEAD_DIM
    assert heads <= LANES
    w_in = w["w_in"]
    wdt = _pad_lanes(w_in[layer, :, w_in.shape[2] - heads:])
    dtb = _pad_lanes(_vec(w["ssd_dt_bias"]))
    alog = _pad_lanes(_vec(w["ssd_A_log"]))
    dvec = _vec(jnp.repeat(w["ssd_D"], SSD_HEAD_DIM))
    return wdt, dtb, alog, dvec


def _ssd_prompt(x2d, proj, bsz, seq, w, layer):
    d_model = x2d.shape[1]
    d_ssd = w["ssd_norm_g"].shape[0]
    d_xbc = w["ssd_conv_b"].shape[0]
    q = SSD_CHUNK
    nc = seq // q
    assert seq % q == 0 and (2 * d_ssd + d_ssd) % d_xbc == 0 and d_model == d_ssd
    z_blk = 2 * d_model // d_ssd
    xbc_blk = (2 * d_model + d_ssd) // d_xbc
    wdt, dtb, alog, dvec = _ssd_head_params(w, layer)
    row = lambda b, c: b * nc + c
    lane_spec = pl.BlockSpec((1, LANES), lambda b, c: (0, 0))
    return pl.pallas_call(
        _ssd_prompt_kernel,
        grid=(bsz, nc),
        in_specs=[pl.BlockSpec((q, d_model), lambda b, c: (row(b, c), 0)),
                  pl.BlockSpec((q, d_ssd), lambda b, c: (row(b, c), z_blk)),
                  pl.BlockSpec((q, d_xbc), lambda b, c: (row(b, c), xbc_blk)),
                  pl.BlockSpec((d_model, LANES), lambda b, c: (0, 0)), lane_spec, lane_spec,
                  pl.BlockSpec((CONV_W, d_xbc), lambda b, c: (0, 0)),
                  pl.BlockSpec((1, d_xbc), lambda b, c: (0, 0)),
                  pl.BlockSpec((1, d_ssd), lambda b, c: (0, 0)),
                  pl.BlockSpec((1, d_ssd), lambda b, c: (0, 0))],
        out_specs=[pl.BlockSpec((q, d_ssd), lambda b, c: (row(b, c), 0)),
                   pl.BlockSpec((1, d_ssd, SSD_STATE), lambda b, c: (b, 0, 0))],
        out_shape=[jax.ShapeDtypeStruct((bsz * seq, d_ssd), BF16),
                   jax.ShapeDtypeStruct((bsz, d_ssd, SSD_STATE), F32)],
        scratch_shapes=[pltpu.VMEM((SUBLANES + q, d_xbc), F32), pltpu.VMEM((d_ssd, SSD_STATE), F32),
                        pltpu.VMEM((q, d_ssd), F32)],
        compiler_params=_params("parallel", "arbitrary"),
        name="ssd_prompt",
    )(x2d, proj, proj, wdt, dtb, alog, w["ssd_conv_w"], _vec(w["ssd_conv_b"]), dvec, _vec(w["ssd_norm_g"]))


def _ssd_sample_prep_kernel(x_ref, xbc_ref, buf_ref, wdt_ref, dtb_ref, alog_ref, cw_ref, cb_ref,
                            xs_ref, xdt_t_ref, b_ref, c_t_ref, dec_ref, xdt_scr):
    n, d_xbc = xbc_ref.shape
    d_ssd = xs_ref.shape[1]
    gw = SSD_GROUPS * SSD_STATE
    conv = cb_ref[...]
    for k in range(CONV_W - 1):
        conv = conv + cw_ref[k:k + 1, :] * buf_ref[:, k * d_xbc:(k + 1) * d_xbc]
    act = _silu(conv + cw_ref[CONV_W - 1:CONV_W, :] * xbc_ref[...])
    xs = act[:, :d_ssd]
    xs_ref[...] = xs
    b_ref[...] = act[:, d_ssd:d_ssd + gw]
    c_t_ref[...] = act[:, d_ssd + gw:].T
    dt = _softplus(_dot(x_ref[...].astype(BF16), wdt_ref[...].astype(BF16)) + dtb_ref[...])
    dec_ref[...] = jnp.exp(dt * (-jnp.exp(alog_ref[...])))
    for h in range(d_ssd // SSD_HEAD_DIM):
        hs = slice(h * SSD_HEAD_DIM, (h + 1) * SSD_HEAD_DIM)
        xdt_scr[:, hs] = xs[:, hs] * dt[:, h:h + 1]
    xdt_t_ref[...] = xdt_scr[...].T.astype(BF16)


def _ssd_sample_state_kernel(dec_ref, h0_ref, xdt_t_ref, b_ref, c_t_ref, xs_ref, z_ref, dvec_ref, ng_ref,
                             hn_ref, y_ref, yt_scr):
    bs, d_ssd, _ = h0_ref.shape
    n = xs_ref.shape[0]
    hpg = d_ssd // SSD_HEAD_DIM // SSD_GROUPS
    grows = hpg * SSD_HEAD_DIM
    step = pl.program_id(0)

    @pl.when(step == 0)
    def _():
        yt_scr[...] = jnp.zeros_like(yt_scr)

    seq_rows = lax.broadcasted_iota(jnp.int32, (n, SSD_STATE), 0)
    seq_lanes = lax.broadcasted_iota(jnp.int32, (SSD_STATE, n), 1)
    for j in range(bs):
        b = step * bs + j
        for g in range(SSD_GROUPS):
            gs = slice(g * SSD_STATE, (g + 1) * SSD_STATE)
            b_sel = jnp.where(seq_rows == b, b_ref[:, gs], 0.0).astype(BF16)
            c_sel = jnp.where(seq_lanes == b, c_t_ref[gs, :], 0.0).astype(BF16)
            outer = _dot(xdt_t_ref[g * grows:(g + 1) * grows, :], b_sel)
            for r in range(hpg):
                h = g * hpg + r
                hs = slice(h * SSD_HEAD_DIM, (h + 1) * SSD_HEAD_DIM)
                hn = dec_ref[b, h] * h0_ref[j, hs, :] + outer[r * SSD_HEAD_DIM:(r + 1) * SSD_HEAD_DIM, :]
                hn_ref[j, hs, :] = hn
                yt_scr[hs, :] += _dot(hn.astype(BF16), c_sel)

    @pl.when(step == pl.num_programs(0) - 1)
    def _():
        y = yt_scr[...].T + dvec_ref[...] * xs_ref[...]
        y_ref[...] = _rms_norm(y * _silu(z_ref[...]), ng_ref[...]).astype(y_ref.dtype)


def _ssd_sample(x2d, proj, buf, h0, w, layer, *, bs):
    n, d_model = x2d.shape
    d_ssd = w["ssd_norm_g"].shape[0]
    d_xbc = w["ssd_conv_b"].shape[0]
    gw = SSD_GROUPS * SSD_STATE
    assert n == SSD_STATE == LANES and n % bs == 0
    z_blk = 2 * d_model // d_ssd
    xbc_blk = (2 * d_model + d_ssd) // d_xbc
    wdt, dtb, alog, dvec = _ssd_head_params(w, layer)
    lane_spec = pl.BlockSpec((1, LANES), lambda i: (0, 0))
    xs, xdt_t, bmat, c_t, dec = pl.pallas_call(
        _ssd_sample_prep_kernel,
        grid=(1,),
        in_specs=[pl.BlockSpec((n, d_model), lambda i: (0, 0)),
                  pl.BlockSpec((n, d_xbc), lambda i: (0, xbc_blk)),
                  pl.BlockSpec((n, (CONV_W - 1) * d_xbc), lambda i: (0, 0)),
                  pl.BlockSpec((d_model, LANES), lambda i: (0, 0)), lane_spec, lane_spec,
                  pl.BlockSpec((CONV_W, d_xbc), lambda i: (0, 0)),
                  pl.BlockSpec((1, d_xbc), lambda i: (0, 0))],
        out_specs=[pl.BlockSpec((n, d_ssd), lambda i: (0, 0)),
                   pl.BlockSpec((d_ssd, n), lambda i: (0, 0)),
                   pl.BlockSpec((n, gw), lambda i: (0, 0)),
                   pl.BlockSpec((gw, n), lambda i: (0, 0)),
                   pl.BlockSpec((n, LANES), lambda i: (0, 0))],
        out_shape=[jax.ShapeDtypeStruct((n, d_ssd), F32), jax.ShapeDtypeStruct((d_ssd, n), BF16),
                   jax.ShapeDtypeStruct((n, gw), F32), jax.ShapeDtypeStruct((gw, n), F32),
                   jax.ShapeDtypeStruct((n, LANES), F32)],
        scratch_shapes=[pltpu.VMEM((n, d_ssd), F32)],
        compiler_params=_params("arbitrary"),
        name="ssd_sample_prep",
    )(x2d, proj, buf.reshape(n, (CONV_W - 1) * d_xbc), wdt, dtb, alog, w["ssd_conv_w"], _vec(w["ssd_conv_b"]))

    const = lambda shape: pl.BlockSpec(shape, lambda i: (0,) * len(shape))
    hn, y = pl.pallas_call(
        _ssd_sample_state_kernel,
        grid=(n // bs,),
        in_specs=[pl.BlockSpec(memory_space=pltpu.SMEM),
                  pl.BlockSpec((bs, d_ssd, SSD_STATE), lambda i: (i, 0, 0)),
                  const((d_ssd, n)), const((n, gw)), const((gw, n)), const((n, d_ssd)),
                  pl.BlockSpec((n, d_ssd), lambda i: (0, z_blk)), const((1, d_ssd)), const((1, d_ssd))],
        out_specs=[pl.BlockSpec((bs, d_ssd, SSD_STATE), lambda i: (i, 0, 0)), const((n, d_ssd))],
        out_shape=[jax.ShapeDtypeStruct((n, d_ssd, SSD_STATE), F32), jax.ShapeDtypeStruct((n, d_ssd), BF16)],
        scratch_shapes=[pltpu.VMEM((d_ssd, n), F32)],
        compiler_params=_params("arbitrary"),
        name="ssd_sample_state",
    )(dec, h0.reshape(n, d_ssd, SSD_STATE), xdt_t, bmat, c_t, xs, proj, dvec, _vec(w["ssd_norm_g"]))
    return y, hn


def _attn_prompt_kernel(q_ref, k_ref, v_ref, o_ref):
    d = q_ref.shape[1]
    hd = d // MEM_HEADS
    scale = 1.0 / math.sqrt(hd)
    for h in range(MEM_HEADS):
        sl = slice(h * hd, (h + 1) * hd)
        s = _dot_nt(q_ref[:, sl], k_ref[:, sl].astype(BF16)) * scale
        e = jnp.exp(s - jnp.max(s, -1, keepdims=True))
        p = e / jnp.sum(e, -1, keepdims=True)
        o_ref[:, sl] = _dot(p.astype(BF16), v_ref[:, sl].astype(BF16)).astype(o_ref.dtype)


def _attn_prompt(q, mk, mv, bsz, seq, mem_len, *, tq):
    d = q.shape[1]
    nt = seq // tq
    assert seq % tq == 0
    kv_spec = pl.BlockSpec((mem_len, d), lambda b, t: (b, 0))
    return pl.pallas_call(
        _attn_prompt_kernel,
        grid=(bsz, nt),
        in_specs=[pl.BlockSpec((tq, d), lambda b, t: (b * nt + t, 0)), kv_spec, kv_spec],
        out_specs=pl.BlockSpec((tq, d), lambda b, t: (b * nt + t, 0)),
        out_shape=jax.ShapeDtypeStruct((bsz * seq, d), BF16),
        compiler_params=_params("parallel", "arbitrary"),
        name="attn_prompt",
    )(q, mk, mv)


def _attn_sample_kernel(q_ref, k_ref, v_ref, o_ref):
    bs, _, heads, hd = k_ref.shape
    scale = 1.0 / math.sqrt(hd)
    for j in range(bs):
        for h in range(heads):
            sl = slice(h * hd, (h + 1) * hd)
            s = jnp.sum(k_ref[j, :, h, :] * q_ref[j, :, sl], -1, keepdims=True) * scale
            e = jnp.exp(s - jnp.max(s, 0, keepdims=True))
            p = e / jnp.sum(e, 0, keepdims=True)
            o_ref[j, :, sl] = jnp.sum(p * v_ref[j, :, h, :], 0, keepdims=True)


def _attn_sample(q, cache_k, cache_v, layer, *, bs):
    n, d = q.shape
    _, nk, mem_len, heads, hd = cache_k.shape
    assert nk == n and n % bs == 0 and heads * hd == d
    kv_spec = pl.BlockSpec((None, bs, mem_len, heads, hd), lambda i: (layer, i, 0, 0, 0))
    q_spec = pl.BlockSpec((bs, 1, d), lambda i: (i, 0, 0))
    return pl.pallas_call(
        _attn_sample_kernel,
        grid=(n // bs,),
        in_specs=[q_spec, kv_spec, kv_spec],
        out_specs=q_spec,
        out_shape=jax.ShapeDtypeStruct((n, 1, d), F32),
        compiler_params=_params("parallel"),
        name="attn_sample",
    )(q.reshape(n, 1, d), cache_k, cache_v).reshape(n, d)


def _top_values(s):
    vals = []
    work = s
    for _ in range(PEER_TOPK):
        m = jnp.max(work, 0, keepdims=True)
        vals.append(m)
        work = jnp.where(work == m, NEG_INF, work)
    return vals


def _count_ge(s, v):
    return jnp.sum(jnp.where(s >= v, 1.0, 0.0), 0, keepdims=True)


def _multiplicities(s, vals):
    return _stack_rows([jnp.sum(jnp.where(s == v, 1.0, 0.0), 0, keepdims=True) for v in vals])


def _stack_rows(rows):
    ridx = lax.broadcasted_iota(jnp.int32, (len(rows), rows[0].shape[1]), 0)
    out = jnp.broadcast_to(rows[0], ridx.shape)
    for i in range(1, len(rows)):
        out = jnp.where(ridx == i, rows[i], out)
    return out


def _peer_route_kernel(q_ref, sk_ref, c1_ref, d1_ref, s2_ref, e2_ref, s1_scr, top_scr, cnt_scr):
    half = sk_ref.shape[3]
    n = q_ref.shape[0]
    row8 = lax.broadcasted_iota(jnp.int32, (SUBLANES, n), 0)
    lo, hi = slice(0, SUBLANES), slice(SUBLANES, PEER_TOPK)
    rows = lambda stacked: [stacked[k:k + 1] for k in range(PEER_TOPK)]

    most = None
    for h in range(PEER_HEADS):
        s1, s2 = [_dot_nt(sk_ref[h, c].astype(BF16), q_ref[:, (2 * h + c) * half:(2 * h + c + 1) * half])
                  for c in range(2)]
        va, vb = _top_values(s1), _top_values(s2)
        s1_scr[h] = s1
        s2_ref[h] = s2
        top_scr[h, 0] = _stack_rows(va)
        top_scr[h, 1] = _stack_rows(vb)
        ge = jnp.maximum(_count_ge(s1, va[-1]), _count_ge(s2, vb[-1]))
        most = ge if most is None else jnp.maximum(most, ge)

    cnt_scr[...] = jnp.ones_like(cnt_scr)

    @pl.when(jnp.max(most) > PEER_TOPK)
    def _():
        for h in range(PEER_HEADS):
            cnt_scr[h, 0] = _multiplicities(s1_scr[h], rows(top_scr[h, 0]))
            cnt_scr[h, 1] = _multiplicities(s2_ref[h], rows(top_scr[h, 1]))

    for h in range(PEER_HEADS):
        s1, s2 = s1_scr[h], s2_ref[h]
        ca_all, cb_all = cnt_scr[h, 0], cnt_scr[h, 1]
        va_all, vb_all = top_scr[h, 0], top_scr[h, 1]
        va, vb = rows(va_all), rows(vb_all)
        blocks = [va[0] + vb_all[lo], va[0] + vb_all[hi]]
        mults = [ca_all[0:1] * cb_all[lo], ca_all[0:1] * cb_all[hi]]
        for k1 in range(1, SUBLANES):
            blocks.append(jnp.where(row8 < PEER_TOPK // (k1 + 1), va[k1] + vb_all[lo], NEG_INF))
            mults.append(ca_all[k1:k1 + 1] * cb_all[lo])
        blocks.append(va_all[hi] + vb[0])
        mults.append(ca_all[hi] * cb_all[0:1])
        top = va[0] + vb[0]
        tau = top
        seen = jnp.zeros_like(top)
        work = blocks
        for _ in range(PEER_TOPK):
            m = jnp.max(functools.reduce(jnp.maximum, work), 0, keepdims=True)
            eq = [blk == m for blk in work]
            tau = jnp.where(seen < PEER_TOPK, m, tau)
            hit = functools.reduce(jnp.add, [jnp.where(q, c, 0.0) for q, c in zip(eq, mults)])
            seen = seen + jnp.sum(hit, 0, keepdims=True)
            work = [jnp.where(q, NEG_INF, blk) for q, blk in zip(eq, work)]
        zs = functools.reduce(jnp.add, [jnp.where(blk >= tau, c * jnp.exp(blk - top), 0.0)
                                        for blk, c in zip(blocks, mults)])
        z = jnp.sum(zs, 0, keepdims=True)
        cut = [jnp.min(jnp.minimum(jnp.where(blocks[0] >= tau, vb_all[lo], jnp.inf),
                                   jnp.where(blocks[1] >= tau, vb_all[hi], jnp.inf)), 0, keepdims=True)]
        for k1 in range(1, SUBLANES):
            cut.append(jnp.min(jnp.where(blocks[k1 + 1] >= tau, vb_all[lo], jnp.inf), 0, keepdims=True))
        cut_tail = jnp.where(blocks[-1] >= tau, vb[0], jnp.inf)
        d1 = jnp.full_like(s1, jnp.inf)
        for k1 in range(PEER_TOPK):
            c = cut[k1] if k1 < SUBLANES else cut_tail[k1 - SUBLANES:k1 - SUBLANES + 1]
            d1 = jnp.where(s1 == va[k1], c, d1)
        c1_ref[h] = jnp.exp(s1 - va[0]) * (1.0 / z)
        d1_ref[h] = d1
        s2_ref[h] = s2
        e2_ref[h] = jnp.exp(s2 - vb[0])


def _peer_route(q, subkeys, layer, *, tn):
    n = q.shape[0]
    _, heads, _, nkeys, half = subkeys.shape
    assert heads == PEER_HEADS and nkeys == PEER_NKEYS and n % tn == 0 and PEER_TOPK == 2 * SUBLANES
    score_spec = pl.BlockSpec((heads, nkeys, tn), lambda i: (0, 0, i))
    score_shape = jax.ShapeDtypeStruct((heads, nkeys, n), F32)
    return pl.pallas_call(
        _peer_route_kernel,
        grid=(n // tn,),
        in_specs=[pl.BlockSpec((tn, q.shape[1]), lambda i: (i, 0)),
                  pl.BlockSpec((None,) + subkeys.shape[1:], lambda i: (layer, 0, 0, 0, 0))],
        out_specs=[score_spec] * 4,
        out_shape=[score_shape] * 4,
        scratch_shapes=[pltpu.VMEM((heads, nkeys, tn), F32), pltpu.VMEM((heads, 2, PEER_TOPK, tn), F32),
                        pltpu.VMEM((heads, 2, PEER_TOPK, tn), F32)],
        compiler_params=_params("parallel"),
        name="peer_route",
    )(q, subkeys)


def _peer_gate_tile(c1_ref, d1_ref, s2_ref, e2_ref, w_ref, e_tile):
    tn, et = w_ref.shape
    for ii in range(et // PEER_NKEYS):
        i1 = e_tile * (et // PEER_NKEYS) + ii
        cut = [d1_ref[h, pl.ds(i1, 1), :] for h in range(PEER_HEADS)]
        fac = [c1_ref[h, pl.ds(i1, 1), :] for h in range(PEER_HEADS)]
        for lt in range(tn // LANES):
            ts = slice(lt * LANES, (lt + 1) * LANES)
            w_t = None
            for h in range(PEER_HEADS):
                hit = s2_ref[h, :, ts] >= cut[h][:, ts]
                term = jnp.where(hit, e2_ref[h, :, ts], 0.0) * fac[h][:, ts]
                w_t = term if w_t is None else w_t + term
            w_ref[ts, ii * PEER_NKEYS:(ii + 1) * PEER_NKEYS] = w_t.T


def _peer_main_kernel(x_ref, c1_ref, d1_ref, s2_ref, e2_ref, u_ref, v_ref, g_ref, b_ref,
                      o_ref, xbf_scr, w_scr, *, alpha):
    e = pl.program_id(1)
    last = pl.num_programs(1) - 1
    route = (c1_ref, d1_ref, s2_ref, e2_ref)

    @pl.when(e == 0)
    def _():
        xbf_scr[...] = x_ref[...].astype(BF16)
        o_ref[...] = jnp.zeros_like(o_ref)
        _peer_gate_tile(*route, w_scr.at[0], 0)

    _peer_gate_tile(*route, w_scr.at[(e + 1) % 2], jnp.minimum(e + 1, last))
    act = _dot_nt(xbf_scr[...], u_ref[...])
    m = (_gelu(act) * w_scr[e % 2]).astype(BF16)
    o_ref[...] += _dot(m, v_ref[...])

    @pl.when(e == last)
    def _():
        o_ref[...] = _layer_norm(alpha * x_ref[...] + o_ref[...], g_ref[...], b_ref[...])


def _cast_kernel(x_ref, o_ref):
    o_ref[...] = x_ref[...].astype(o_ref.dtype)


def _to_bf16(w, layer, *, rows):
    _, r, c = w.shape
    assert r % rows == 0
    return pl.pallas_call(
        _cast_kernel,
        grid=(r // rows,),
        in_specs=[pl.BlockSpec((None, rows, c), lambda i: (layer, i, 0))],
        out_specs=pl.BlockSpec((rows, c), lambda i: (i, 0)),
        out_shape=jax.ShapeDtypeStruct((r, c), BF16),
        compiler_params=_params("parallel"),
        name="to_bf16",
    )(w)


def _peer_main(x, route, u, v, g, b, *, alpha, tn, et):
    n, d = x.shape
    n_exp = u.shape[0]
    assert n_exp == PEER_NKEYS * PEER_NKEYS and n_exp % et == 0 and et % PEER_NKEYS == 0
    assert n % tn == 0 and tn % LANES == 0 and u.dtype == BF16 and v.dtype == BF16
    once = pl.Buffered(1)
    score_spec = pl.BlockSpec((PEER_HEADS, PEER_NKEYS, tn), lambda i, e: (0, 0, i), pipeline_mode=once)
    expert_spec = pl.BlockSpec((et, d), lambda i, e: (e, 0))
    return pl.pallas_call(
        functools.partial(_peer_main_kernel, alpha=alpha),
        grid=(n // tn, n_exp // et),
        in_specs=[pl.BlockSpec((tn, d), lambda i, e: (i, 0), pipeline_mode=once),
                  score_spec, score_spec, score_spec, score_spec, expert_spec, expert_spec,
                  pl.BlockSpec((1, d), lambda i, e: (0, 0)),
                  pl.BlockSpec((1, d), lambda i, e: (0, 0))],
        out_specs=pl.BlockSpec((tn, d), lambda i, e: (i, 0), pipeline_mode=once),
        out_shape=jax.ShapeDtypeStruct((n, d), F32),
        scratch_shapes=[pltpu.VMEM((tn, d), BF16), pltpu.VMEM((2, tn, et), F32)],
        compiler_params=_params("parallel", "arbitrary"),
        name="peer_main",
    )(x, *route, u, v, _vec(g), _vec(b))


def _tiles(n_tokens):
    return dict(mm_rows=min(n_tokens, 2048), mm_cols=512, ln_rows=min(n_tokens, 1024), ln_k=512,
                peer_rows=min(n_tokens, 512), peer_experts=512, route_rows=LANES,
                lru_rows=256, attn_rows=512, ssd_state_seqs=4, attn_seqs=2)


def _token_stages(x1_in, attn_fn, w, layer, alpha, t):
    q = _mm(x1_in, w["mem_wq"], layer, out_dtype=attn_fn.q_dtype, tm=t["mm_rows"], tn=t["mm_cols"],
            name="mem_q")
    o = attn_fn(q)
    x2 = _mm_ln([o], w["mem_wo"], layer, x1_in, w["ln2_g"], w["ln2_b"], alpha=alpha, tm=t["ln_rows"],
                tk=t["ln_k"], name="mem_out_ln")
    pq = _mm(x2, w["peer_wq"], layer, out_dtype=BF16, tm=t["mm_rows"], tn=t["mm_cols"], name="peer_q")
    route = _peer_route(pq, w["peer_subkeys"], layer, tn=t["route_rows"])
    return _peer_main(x2, route, w["peer_u_bf16"], w["peer_v_bf16"], w["ln3_g"], w["ln3_b"], alpha=alpha,
                      tn=t["peer_rows"], et=t["peer_experts"])


class _AttnFn:
    def __init__(self, fn, q_dtype):
        self.fn, self.q_dtype = fn, q_dtype

    def __call__(self, q):
        return self.fn(q)


def _prompt_layer(x, mem, w, layer, alpha):
    bsz, seq, d = x.shape
    x2d = x.reshape(bsz * seq, d)
    d_lru = w["lru_lambda"].shape[0]
    d_ssd = w["ssd_norm_g"].shape[0]
    d_xbc = w["ssd_conv_b"].shape[0]
    n_proj = 2 * d_lru + d_ssd + d_xbc
    assert seq >= CONV_W - 1
    t = _tiles(bsz * seq)
    proj = _mm(x2d, w["w_in"], layer, ncols=n_proj, tm=t["mm_rows"], tn=t["mm_cols"], name="in_proj")
    y_lru, lru_h = _lru_prompt(proj, bsz, seq, w, layer, tt=t["lru_rows"])
    y_ssd, ssd_h = _ssd_prompt(x2d, proj, bsz, seq, w, layer)
    tail = proj.reshape(bsz, seq, n_proj)[:, seq - (CONV_W - 1):]
    lru_conv = tail[:, :, :d_lru]
    ssd_conv = tail[:, :, n_proj - d_xbc:]
    x1 = _mm_ln([y_lru, y_ssd], w["w_out"], layer, x2d, w["ln1_g"], w["ln1_b"], alpha=alpha,
                tm=t["ln_rows"], tk=t["ln_k"], name="mix_out_ln")
    mem_len = mem.shape[1]
    mem2d = mem.reshape(bsz * mem_len, d)
    tmem = _tiles(bsz * mem_len)
    mk = _mm(mem2d, w["mem_wk"], layer, tm=tmem["mm_rows"], tn=tmem["mm_cols"], name="mem_k")
    mv = _mm(mem2d, w["mem_wv"], layer, tm=tmem["mm_rows"], tn=tmem["mm_cols"], name="mem_v")
    attn = _AttnFn(lambda q: _attn_prompt(q, mk, mv, bsz, seq, mem_len, tq=t["attn_rows"]), BF16)
    y = _token_stages(x1, attn, w, layer, alpha, t)
    hd = d // MEM_HEADS
    states = (lru_conv, lru_h.reshape(bsz, d_lru), ssd_conv,
              ssd_h.reshape(bsz, d_ssd // SSD_HEAD_DIM, SSD_HEAD_DIM, SSD_STATE),
              mk.reshape(bsz, mem_len, MEM_HEADS, hd), mv.reshape(bsz, mem_len, MEM_HEADS, hd))
    return y.reshape(bsz, seq, d), states


def _sample_layer(x, ck, cv, lru_buf, lru_h, ssd_buf, ssd_h, w, layer, alpha):
    n, seq, d = x.shape
    assert seq == 1
    x2d = x.reshape(n, d)
    d_lru = w["lru_lambda"].shape[0]
    d_ssd = w["ssd_norm_g"].shape[0]
    d_xbc = w["ssd_conv_b"].shape[0]
    n_proj = 2 * d_lru + d_ssd + d_xbc
    t = _tiles(n)
    proj = _mm(x2d, w["w_in"], layer, ncols=n_proj, tm=t["mm_rows"], tn=t["mm_cols"], name="in_proj_s")
    y_lru, lru_hn = _lru_sample(proj, lru_buf, lru_h, w, layer)
    y_ssd, ssd_hn = _ssd_sample(x2d, proj, ssd_buf, ssd_h, w, layer, bs=t["ssd_state_seqs"])
    lru_conv = jnp.concatenate([lru_buf[:, 1:], proj[:, None, :d_lru]], 1)
    ssd_conv = jnp.concatenate([ssd_buf[:, 1:], proj[:, None, n_proj - d_xbc:]], 1)
    x1 = _mm_ln([y_lru, y_ssd], w["w_out"], layer, x2d, w["ln1_g"], w["ln1_b"], alpha=alpha,
                tm=t["ln_rows"], tk=t["ln_k"], name="mix_out_ln_s")
    attn = _AttnFn(lambda q: _attn_sample(q, ck, cv, layer, bs=t["attn_seqs"]), F32)
    y = _token_stages(x1, attn, w, layer, alpha, t)
    states = (lru_conv, lru_hn, ssd_conv, ssd_hn.reshape(ssd_h.shape))
    return y.reshape(n, 1, d), states


def kernel(x_prompt, x_sample, mem_prompt, state_lru_conv, state_lru_h, state_ssd_conv, state_ssd_h,
           cache_mem_k, cache_mem_v, w_in, lru_conv_w, lru_conv_b, lru_gate_a_w, lru_gate_a_b,
           lru_gate_x_w, lru_gate_x_b, lru_lambda, lru_norm_g, ssd_conv_w, ssd_conv_b, ssd_dt_bias,
           ssd_A_log, ssd_D, ssd_norm_g, w_out, ln1_g, ln1_b, mem_wq, mem_wk, mem_wv, mem_wo,
           ln2_g, ln2_b, peer_wq, peer_subkeys, peer_u, peer_v, ln3_g, ln3_b):
    depth = w_in.shape[0]
    alpha = (2.0 * depth) ** 0.25
    weights = dict(w_in=w_in, lru_conv_w=lru_conv_w, lru_conv_b=lru_conv_b, lru_gate_a_w=lru_gate_a_w,
                   lru_gate_a_b=lru_gate_a_b, lru_gate_x_w=lru_gate_x_w, lru_gate_x_b=lru_gate_x_b,
                   lru_lambda=lru_lambda, lru_norm_g=lru_norm_g, ssd_conv_w=ssd_conv_w,
                   ssd_conv_b=ssd_conv_b, ssd_dt_bias=ssd_dt_bias, ssd_A_log=ssd_A_log, ssd_D=ssd_D,
                   ssd_norm_g=ssd_norm_g, w_out=w_out, ln1_g=ln1_g, ln1_b=ln1_b, mem_wq=mem_wq,
                   mem_wk=mem_wk, mem_wv=mem_wv, mem_wo=mem_wo, ln2_g=ln2_g, ln2_b=ln2_b,
                   peer_wq=peer_wq, peer_subkeys=peer_subkeys, peer_u=peer_u, peer_v=peer_v,
                   ln3_g=ln3_g, ln3_b=ln3_b)
    yp, ys = x_prompt, x_sample
    p_states, s_states = [], []
    stacked = ("w_in", "w_out", "mem_wq", "mem_wk", "mem_wv", "mem_wo", "peer_wq", "peer_subkeys",
               "peer_u", "peer_v", "lru_gate_a_w", "lru_gate_x_w")
    for l in range(depth):
        w = {name: (p if name in stacked else p[l]) for name, p in weights.items()}
        w["peer_u_bf16"] = _to_bf16(peer_u, l, rows=CAST_ROWS)
        w["peer_v_bf16"] = _to_bf16(peer_v, l, rows=CAST_ROWS)
        yp, st = _prompt_layer(yp, mem_prompt, w, l, alpha)
        p_states.append(st)
        ys, st = _sample_layer(ys, cache_mem_k, cache_mem_v, state_lru_conv[l], state_lru_h[l],
                               state_ssd_conv[l], state_ssd_h[l], w, l, alpha)
        s_states.append(st)

    def stack(states, i):
        return states[0][i][None] if depth == 1 else jnp.stack([st[i] for st in states])

    return (yp, ys) + tuple(stack(p_states, i) for i in range(6)) + tuple(stack(s_states, i) for i in range(4))
```
